```python
import jax, jax.numpy as jnp
from jax import lax
import numpy as np

D_MODEL = 1024
BATCH = 8
SEQ = 8192
DEPTH = 1
DEC_BATCH = 8
DEC_SEQ = 2048
PAST_LEN = 128

GRID_W = 64
BLOCK = 128
WINDOW = 128
ROPE_THETA = 10000.0
EPS = 1e-6
NEG_INF = -1e30

A_HEADS = 16
A_KV = 4
A_HD = 64
A_Q = A_HEADS * A_HD
A_KVW = A_KV * A_HD
B_HEADS = 8
B_KV = 2
B_HD = 128
B_Q = B_HEADS * B_HD
B_KVW = B_KV * B_HD
IN_COLS = A_Q + 2 * A_KVW + B_Q + 2 * B_KVW
IN_SPLITS = (A_Q, A_Q + A_KVW, A_Q + 2 * A_KVW, A_Q + 2 * A_KVW + B_Q, A_Q + 2 * A_KVW + B_Q + B_KVW)
N_BRANCH = 2
PK_HEADS = 8
N_KEYS = 128
N_EXPERTS = N_KEYS * N_KEYS
PK_DIM = 256
PK_HALF = PK_DIM // 2
PK_TOPK = 16
PEER_CHUNK = 128

kernel_name = 'hybrid_window_axial_peer_encoder'


def rmsnorm(x, g):
    xf = x.astype(jnp.float32)
    y = xf * lax.rsqrt(jnp.mean(xf * xf, axis=-1, keepdims=True) + EPS) * g.astype(jnp.float32)
    return y.astype(x.dtype)


def rope_cos_sin(pos, dim):
    inv = ROPE_THETA ** (-jnp.arange(0, dim, 2, dtype=jnp.float32) / dim)
    ang = pos.astype(jnp.float32)[:, None] * inv[None, :]
    ang = jnp.concatenate([ang, ang], axis=-1)
    return jnp.cos(ang), jnp.sin(ang)


def apply_rope(x, cos, sin):
    half = x.shape[-1] // 2
    rot = jnp.concatenate([-x[..., half:], x[..., :half]], axis=-1)
    return x * cos[None, :, None, :] + rot * sin[None, :, None, :]


def window_attention(q, k, v, sink):
    B, S = q.shape[0], q.shape[1]
    nb = S // BLOCK
    G = A_HEADS // A_KV
    qb = q.reshape(B, nb, BLOCK, A_KV, G, A_HD).transpose(1, 0, 2, 3, 4, 5)
    pad = ((0, 0), (BLOCK, BLOCK), (0, 0), (0, 0))
    kp = jnp.pad(k, pad)
    vp = jnp.pad(v, pad)
    scale = A_HD ** -0.5
    sink_b = sink.astype(jnp.float32).reshape(A_KV, G)[None, :, :, None, None]

    def one_block(args):
        i, qi = args
        start = i * BLOCK
        kb = lax.dynamic_slice_in_dim(kp, start, 3 * BLOCK, axis=1)
        vb = lax.dynamic_slice_in_dim(vp, start, 3 * BLOCK, axis=1)
        s = jnp.einsum('bqkgd,bjkd->bkgqj', qi, kb).astype(jnp.float32) * scale
        qpos = start + jnp.arange(BLOCK)
        kpos = start - BLOCK + jnp.arange(3 * BLOCK)
        mask = (jnp.abs(qpos[:, None] - kpos[None, :]) <= WINDOW) & (kpos[None, :] >= 0) & (kpos[None, :] < S)
        s = jnp.where(mask[None, None, None], s, NEG_INF)
        sk = jnp.broadcast_to(sink_b, s.shape[:-1] + (1,))
        p = jax.nn.softmax(jnp.concatenate([s, sk], axis=-1), axis=-1)[..., :-1]
        return jnp.einsum('bkgqj,bjkd->bqkgd', p.astype(vb.dtype), vb)

    o = lax.map(one_block, (jnp.arange(nb), qb))
    return o.transpose(1, 0, 2, 3, 4, 5).reshape(B, S, A_Q)


def dense_attention(q, k, v):
    B, S = q.shape[0], q.shape[1]
    nb = S // BLOCK
    G = B_HEADS // B_KV
    qb = q.reshape(B, nb, BLOCK, B_KV, G, B_HD).transpose(1, 0, 2, 3, 4, 5)
    scale = B_HD ** -0.5

    def one_block(qi):
        s = jnp.einsum('bqkgd,bjkd->bkgqj', qi, k).astype(jnp.float32) * scale
        p = jax.nn.softmax(s, axis=-1)
        return jnp.einsum('bkgqj,bjkd->bqkgd', p.astype(v.dtype), v)

    o = lax.map(one_block, qb)
    return o.transpose(1, 0, 2, 3, 4, 5).reshape(B, S, B_Q)


def parallel_mixer(h, w_in, w_gate, b_gate, sink, q_norm_g, k_norm_g, w_oa, w_ob, w_out):
    B, S, _ = h.shape
    proj = jnp.einsum('bsd,dc->bsc', h, w_in)
    qa, ka, va, qb, kb, vb = jnp.split(proj, IN_SPLITS, axis=-1)
    qa = qa.reshape(B, S, A_HEADS, A_HD)
    ka = ka.reshape(B, S, A_KV, A_HD)
    va = va.reshape(B, S, A_KV, A_HD)
    cos_a, sin_a = rope_cos_sin(jnp.arange(S), A_HD)
    oa = window_attention(apply_rope(qa, cos_a, sin_a), apply_rope(ka, cos_a, sin_a), va, sink)
    qb = rmsnorm(qb.reshape(B, S, B_HEADS, B_HD), q_norm_g)
    kb = rmsnorm(kb.reshape(B, S, B_KV, B_HD), k_norm_g)
    vb = vb.reshape(B, S, B_KV, B_HD)
    rows = S // GRID_W
    row = jnp.repeat(jnp.arange(rows), GRID_W)
    col = jnp.tile(jnp.arange(GRID_W), rows)
    half = B_HD // 2
    cr, sr = rope_cos_sin(row, half)
    cc, sc = rope_cos_sin(col, half)
    qb = jnp.concatenate([apply_rope(qb[..., :half], cr, sr), apply_rope(qb[..., half:], cc, sc)], axis=-1)
    kb = jnp.concatenate([apply_rope(kb[..., :half], cr, sr), apply_rope(kb[..., half:], cc, sc)], axis=-1)
    ob = dense_attention(qb, kb, vb)
    gates = jax.nn.sigmoid((jnp.einsum('bsd,dc->bsc', h, w_gate) + b_gate).astype(jnp.float32))
    ga, gb = jnp.split(gates, N_BRANCH, axis=-1)
    ya = jnp.einsum('bsc,cd->bsd', oa, w_oa).astype(jnp.float32)
    yb = jnp.einsum('bsc,cd->bsd', ob, w_ob).astype(jnp.float32)
    merged = (ga * ya + gb * yb).astype(h.dtype)
    return jnp.einsum('bsd,de->bse', merged, w_out)


def peer(h, w_query, sub_keys, expert_u, expert_v):
    B, S, D = h.shape
    T = B * S
    hc = h.reshape(T // PEER_CHUNK, PEER_CHUNK, D)

    def one_chunk(xc):
        qry = jnp.einsum('cd,dq->cq', xc, w_query).reshape(PEER_CHUNK, PK_HEADS, 2, PK_HALF)
        sc = jnp.einsum('chpd,hpnd->chpn', qry, sub_keys).astype(jnp.float32)
        s_top, i_top = lax.top_k(sc, PK_TOPK)
        cand = (s_top[:, :, 0, :, None] + s_top[:, :, 1, None, :]).reshape(PEER_CHUNK, PK_HEADS, PK_TOPK * PK_TOPK)
        cand_idx = (i_top[:, :, 0, :, None] * N_KEYS + i_top[:, :, 1, None, :]).reshape(PEER_CHUNK, PK_HEADS, PK_TOPK * PK_TOPK)
        best_s, pos = lax.top_k(cand, PK_TOPK)
        idx = jnp.take_along_axis(cand_idx, pos, axis=-1)
        g = jax.nn.softmax(best_s, axis=-1)
        u = expert_u[idx]
        v = expert_v[idx]
        act = jax.nn.gelu(jnp.einsum('chkd,cd->chk', u, xc).astype(jnp.float32), approximate=False)
        return jnp.einsum('chk,chkd->cd', (g * act).astype(v.dtype), v)

    out = lax.map(one_chunk, hc)
    return out.reshape(B, S, D)


def trunk(x, g_mix, w_in, w_gate, b_gate, sink, q_norm_g, k_norm_g, w_oa, w_ob, w_out,
          g_ffn, w_query, sub_keys, expert_u, expert_v, g_final):
    for l in range(DEPTH):
        h = rmsnorm(x, g_mix[l])
        x = x + parallel_mixer(h, w_in[l], w_gate[l], b_gate[l], sink[l], q_norm_g[l], k_norm_g[l],
                               w_oa[l], w_ob[l], w_out[l])
        h = rmsnorm(x, g_ffn[l])
        x = x + peer(h, w_query[l], sub_keys[l], expert_u[l], expert_v[l])
    return rmsnorm(x, g_final)


def setup_inputs(seed: int = 0) -> dict:
    key = jax.random.key(seed)
    ks = jax.random.split(key, 20)
    f32 = jnp.float32
    D = D_MODEL

    def nrm(k, shape, scale):
        return jax.random.normal(k, shape, f32) * scale

    return {
        'x_prompt': nrm(ks[0], (BATCH, SEQ, D), 1.0),
        'x_sample': nrm(ks[1], (DEC_BATCH, DEC_SEQ, D), 1.0),
        'g_mix': 1.0 + nrm(ks[2], (DEPTH, D), 0.02),
        'w_in': nrm(ks[3], (DEPTH, D, IN_COLS), D ** -0.5),
        'w_gate': nrm(ks[4], (DEPTH, D, N_BRANCH * D), D ** -0.5),
        'b_gate': nrm(ks[5], (DEPTH, N_BRANCH * D), 0.02),
        'sink': nrm(ks[6], (DEPTH, A_HEADS), 0.5),
        'q_norm_g': 1.0 + nrm(ks[7], (DEPTH, B_HD), 0.02),
        'k_norm_g': 1.0 + nrm(ks[8], (DEPTH, B_HD), 0.02),
        'w_oa': nrm(ks[9], (DEPTH, A_Q, D), A_Q ** -0.5),
        'w_ob': nrm(ks[10], (DEPTH, B_Q, D), B_Q ** -0.5),
        'w_out': nrm(ks[11], (DEPTH, D, D), D ** -0.5),
        'g_ffn': 1.0 + nrm(ks[12], (DEPTH, D), 0.02),
        'w_query': nrm(ks[13], (DEPTH, D, PK_HEADS * PK_DIM), D ** -0.5),
        'sub_keys': nrm(ks[14], (DEPTH, PK_HEADS, 2, N_KEYS, PK_HALF), PK_HALF ** -0.5),
        'expert_u': nrm(ks[15], (DEPTH, N_EXPERTS, D), D ** -0.5),
        'expert_v': nrm(ks[16], (DEPTH, N_EXPERTS, D), PK_HEADS ** -0.5),
        'g_final': 1.0 + nrm(ks[17], (D,), 0.02),
    }


def reference(x_prompt, x_sample, g_mix, w_in, w_gate, b_gate, sink, q_norm_g, k_norm_g, w_oa, w_ob, w_out,
              g_ffn, w_query, sub_keys, expert_u, expert_v, g_final):
    y_prompt = trunk(x_prompt, g_mix, w_in, w_gate, b_gate, sink, q_norm_g, k_norm_g, w_oa, w_ob, w_out,
                     g_ffn, w_query, sub_keys, expert_u, expert_v, g_final)
    y_sample = trunk(x_sample, g_mix, w_in, w_gate, b_gate, sink, q_norm_g, k_norm_g, w_oa, w_ob, w_out,
                     g_ffn, w_query, sub_keys, expert_u, expert_v, g_final)
    return (y_prompt, y_sample)
```

```python
import functools

import jax
import jax.numpy as jnp
from jax import lax
from jax.experimental import pallas as pl
from jax.experimental.pallas import tpu as pltpu

F32 = jnp.float32
BF16 = jnp.bfloat16
I32 = jnp.int32

D_MODEL = 1024
EPS = 1e-6
NEG_INF = -1e30
ROPE_THETA = 10000.0
GRID_W = 64
WINDOW = 128
A_HEADS, A_KV, A_HD = 16, 4, 64
B_HEADS, B_KV, B_HD = 8, 2, 128
A_Q, A_KVW = A_HEADS * A_HD, A_KV * A_HD
B_Q, B_KVW = B_HEADS * B_HD, B_KV * B_HD
IN_COLS = A_Q + 2 * A_KVW + B_Q + 2 * B_KVW
PK_HEADS, N_KEYS, PK_TOPK = 8, 128, 16
PK_DIM = 256
N_PICKS = PK_HEADS * PK_TOPK
LANES = 128
SUBLANES = 8
ROW_WORDS = D_MODEL // 2
ROW_SUB = ROW_WORDS // LANES
TILE_STRIDE = N_PICKS + SUBLANES
VMEM_LIMIT = 56 * 1024 * 1024


def _cparams(sem):
    return pltpu.CompilerParams(dimension_semantics=sem, vmem_limit_bytes=VMEM_LIMIT)


def _const_spec(shape):
    nd = len(shape)
    return pl.BlockSpec(shape, lambda *_: (0,) * nd, pipeline_mode=pl.Buffered(1))


def _rms(x):
    return x * lax.rsqrt(jnp.mean(x * x, axis=-1, keepdims=True) + EPS)


def _inproj_kernel(x_ref, g_ref, w_ref, cs_ref, qn_ref, kn_ref,
                   qa_ref, ka_ref, va_ref, qb_ref, kb_ref, vb_ref):
    tb = x_ref.shape[0]
    hb = (_rms(x_ref[...]) * g_ref[...]).astype(BF16)
    lane = lax.broadcasted_iota(I32, (tb, LANES), 1)
    first = (lane % 64) < 32
    cos_a, sin_a, cos_b, sin_b = cs_ref[0], cs_ref[1], cs_ref[2], cs_ref[3]

    def rope(xc, cos, sin_signed):
        rot = jnp.where(first, pltpu.roll(xc, 96, 1), pltpu.roll(xc, 32, 1))
        return xc * cos + rot * sin_signed

    def proj(c0, width):
        return jnp.dot(hb, w_ref[:, c0:c0 + width], preferred_element_type=F32)

    for j in range(A_Q // 256):
        y = proj(j * 256, 256)
        for s in range(2):
            yc = rope(y[:, s * LANES:(s + 1) * LANES], cos_a, sin_a) * (A_HD ** -0.5)
            qa_ref[:, j * 256 + s * LANES:j * 256 + (s + 1) * LANES] = yc.astype(BF16)
    y = proj(A_Q, 256)
    for s in range(2):
        ka_ref[:, s * LANES:(s + 1) * LANES] = rope(y[:, s * LANES:(s + 1) * LANES], cos_a, sin_a).astype(BF16)
    va_ref[...] = proj(A_Q + A_KVW, 256).astype(BF16)
    base = A_Q + 2 * A_KVW
    for j in range(B_Q // 256):
        y = proj(base + j * 256, 256)
        for s in range(2):
            yc = _rms(y[:, s * LANES:(s + 1) * LANES]) * qn_ref[...]
            yc = rope(yc, cos_b, sin_b) * (B_HD ** -0.5)
            qb_ref[:, j * 256 + s * LANES:j * 256 + (s + 1) * LANES] = yc.astype(BF16)
    y = proj(base + B_Q, 256)
    for s in range(2):
        yc = _rms(y[:, s * LANES:(s + 1) * LANES]) * kn_ref[...]
        kb_ref[:, s * LANES:(s + 1) * LANES] = rope(yc, cos_b, sin_b).astype(BF16)
    vb_ref[...] = proj(base + B_Q + B_KVW, 256).astype(BF16)


def _inproj(x2d, g_mix, w_in, cs, qn, kn, seq, tb):
    t = x2d.shape[0]
    ns = seq // tb
    row = lambda w: pl.BlockSpec((tb, w), lambda i: (i, 0))
    out = lambda w: jax.ShapeDtypeStruct((t, w), BF16)
    return pl.pallas_call(
        _inproj_kernel,
        grid=(t // tb,),
        in_specs=[row(D_MODEL), _const_spec((1, D_MODEL)), _const_spec((D_MODEL, IN_COLS)),
                  pl.BlockSpec((4, tb, LANES), lambda i: (0, i % ns, 0)),
                  _const_spec((1, LANES)), _const_spec((1, LANES))],
        out_specs=[row(A_Q), row(A_KVW), row(A_KVW), row(B_Q), row(B_KVW), row(B_KVW)],
        out_shape=[out(A_Q), out(A_KVW), out(A_KVW), out(B_Q), out(B_KVW), out(B_KVW)],
        compiler_params=_cparams(("parallel",)),
        name="inproj",
    )(x2d, g_mix, w_in, cs, qn, kn)


def _window_kernel(sink_ref, q_ref, kp_ref, kc_ref, kn_ref, vp_ref, vc_ref, vn_ref, o_ref):
    i = pl.program_id(1)
    nb = pl.num_programs(1)
    blk = q_ref.shape[0]
    r = lax.broadcasted_iota(I32, (blk, 3 * blk), 0)
    c = lax.broadcasted_iota(I32, (blk, 3 * blk), 1)
    rel = c - blk - r
    lo_ok = jnp.where(i > 0, 0, blk)
    hi_ok = jnp.where(i < nb - 1, 3 * blk, 2 * blk)
    ok = (jnp.abs(rel) <= WINDOW) & (c >= lo_ok) & (c < hi_ok)
    k3 = jnp.concatenate([kp_ref[...], kc_ref[...], kn_ref[...]], axis=0)
    v3 = jnp.concatenate([vp_ref[...], vc_ref[...], vn_ref[...]], axis=0)
    group = A_HEADS // A_KV
    outs = []
    for h in range(A_HEADS):
        kv = h // group
        qh = q_ref[:, h * A_HD:(h + 1) * A_HD]
        kh = k3[:, kv * A_HD:(kv + 1) * A_HD]
        vh = v3[:, kv * A_HD:(kv + 1) * A_HD]
        s = lax.dot_general(qh, kh, (((1,), (1,)), ((), ())), preferred_element_type=F32)
        s = jnp.where(ok, s, NEG_INF)
        sk = sink_ref[h]
        m = jnp.maximum(jnp.max(s, axis=1, keepdims=True), sk)
        p = jnp.exp(s - m)
        den = jnp.sum(p, axis=1, keepdims=True) + jnp.exp(sk - m)
        o = jnp.dot(p.astype(BF16), vh, preferred_element_type=F32) / den
        outs.append(o.astype(BF16))
    o_ref[...] = jnp.concatenate(outs, axis=1)


def _window(sink, qa, ka, va, blk=128):
    b, s, _ = qa.shape
    nb = s // blk
    qspec = pl.BlockSpec((None, blk, A_Q), lambda bi, i: (bi, i, 0))
    prev = pl.BlockSpec((None, blk, A_KVW), lambda bi, i: (bi, jnp.maximum(i - 1, 0), 0))
    cur = pl.BlockSpec((None, blk, A_KVW), lambda bi, i: (bi, i, 0))
    nxt = pl.BlockSpec((None, blk, A_KVW), lambda bi, i: (bi, jnp.minimum(i + 1, nb - 1), 0))
    return pl.pallas_call(
        _window_kernel,
        grid=(b, nb),
        in_specs=[pl.BlockSpec(memory_space=pltpu.SMEM), qspec, prev, cur, nxt, prev, cur, nxt],
        out_specs=qspec,
        out_shape=jax.ShapeDtypeStruct((b, s, A_Q), BF16),
        compiler_params=_cparams(("parallel", "parallel")),
        name="window",
    )(sink, qa, ka, ka, ka, va, va, va)


def _dense_kernel(q_ref, k_ref, v_ref, o_ref, m_sc, l_sc, acc_sc, *, tk):
    tq = q_ref.shape[0]
    group = B_HEADS // B_KV
    q4 = jnp.concatenate([q_ref[:, g * B_HD:(g + 1) * B_HD] for g in range(group)], axis=0)
    m_sc[...] = jnp.full(m_sc.shape, -jnp.inf, F32)
    l_sc[...] = jnp.zeros(l_sc.shape, F32)
    acc_sc[...] = jnp.zeros(acc_sc.shape, F32)

    def body(j, carry):
        off = pl.multiple_of(j * tk, tk)
        k = k_ref[pl.ds(off, tk), :]
        v = v_ref[pl.ds(off, tk), :]
        s = lax.dot_general(q4, k, (((1,), (1,)), ((), ())), preferred_element_type=F32)
        m_prev = m_sc[...]
        m_new = jnp.maximum(m_prev, jnp.max(s, axis=1, keepdims=True))
        alpha = jnp.exp(m_prev - m_new)
        p = jnp.exp(s - m_new)
        l_sc[...] = alpha * l_sc[...] + jnp.sum(p, axis=1, keepdims=True)
        acc_sc[...] = alpha * acc_sc[...] + jnp.dot(p.astype(BF16), v, preferred_element_type=F32)
        m_sc[...] = m_new
        return carry

    lax.fori_loop(0, k_ref.shape[0] // tk, body, 0)
    o = acc_sc[...] / l_sc[...]
    for g in range(group):
        o_ref[:, g * B_HD:(g + 1) * B_HD] = o[g * tq:(g + 1) * tq].astype(BF16)


def _dense(qb, kb, vb, tq=128, tk=512):
    b, s, _ = qb.shape
    group = B_HEADS // B_KV
    gw = group * B_HD
    qspec = pl.BlockSpec((None, tq, gw), lambda bi, kv, i: (bi, i, kv))
    kspec = pl.BlockSpec((None, s, B_HD), lambda bi, kv, i: (bi, 0, kv))
    return pl.pallas_call(
        functools.partial(_dense_kernel, tk=min(tk, s)),
        grid=(b, B_KV, s // tq),
        in_specs=[qspec, kspec, kspec],
        out_specs=qspec,
        out_shape=jax.ShapeDtypeStruct((b, s, B_Q), BF16),
        scratch_shapes=[pltpu.VMEM((group * tq, 1), F32), pltpu.VMEM((group * tq, 1), F32),
                        pltpu.VMEM((group * tq, B_HD), F32)],
        compiler_params=_cparams(("parallel", "parallel", "parallel")),
        name="dense",
    )(qb, kb, vb)


def _mixout_kernel(x_ref, oa_ref, ob_ref, gmix_ref, wg_ref, bg_ref, woa_ref, wob_ref, wout_ref,
                   gffn_ref, wq_ref, x2_ref, h2_ref, qry_ref):
    x = x_ref[...]
    hb = (_rms(x) * gmix_ref[...]).astype(BF16)
    gates = jax.nn.sigmoid(jnp.dot(hb, wg_ref[...], preferred_element_type=F32) + bg_ref[...])
    ya = jnp.dot(oa_ref[...], woa_ref[...], preferred_element_type=F32)
    yb = jnp.dot(ob_ref[...], wob_ref[...], preferred_element_type=F32)
    merged = (gates[:, :D_MODEL] * ya + gates[:, D_MODEL:] * yb).astype(BF16)
    x2 = x + jnp.dot(merged, wout_ref[...], preferred_element_type=F32)
    x2_ref[...] = x2
    h2 = _rms(x2) * gffn_ref[...]
    h2_ref[...] = h2
    qry_ref[...] = jnp.dot(h2.astype(BF16), wq_ref[...], preferred_element_type=F32).astype(BF16)


def _mixout(x2d, oa, ob, gmix, wg, bg, woa, wob, wout, gffn, wq, tb):
    t = x2d.shape[0]
    row = lambda w: pl.BlockSpec((tb, w), lambda i: (i, 0))
    qw = PK_HEADS * PK_DIM
    return pl.pallas_call(
        _mixout_kernel,
        grid=(t // tb,),
        in_specs=[row(D_MODEL), row(A_Q), row(B_Q), _const_spec((1, D_MODEL)),
                  _const_spec((D_MODEL, 2 * D_MODEL)), _const_spec((1, 2 * D_MODEL)),
                  _const_spec((A_Q, D_MODEL)), _const_spec((B_Q, D_MODEL)), _const_spec((D_MODEL, D_MODEL)),
                  _const_spec((1, D_MODEL)), _const_spec((D_MODEL, qw))],
        out_specs=[row(D_MODEL), row(D_MODEL), row(qw)],
        out_shape=[jax.ShapeDtypeStruct((t, D_MODEL), F32), jax.ShapeDtypeStruct((t, D_MODEL), F32),
                   jax.ShapeDtypeStruct((t, qw), BF16)],
        compiler_params=_cparams(("parallel",)),
        name="mixout",
    )(x2d, oa, ob, gmix, wg, bg, woa, wob, wout, gffn, wq)


def _top_rows(vals, k):
    n = vals.shape[0]
    rows = lax.broadcasted_iota(I32, vals.shape, 0).astype(F32)
    top_v, top_i = [], []
    for _ in range(k):
        m = jnp.max(vals, axis=0, keepdims=True)
        idx = jnp.min(jnp.where(vals == m, rows, float(n)), axis=0, keepdims=True)
        top_v.append(m)
        top_i.append(idx)
        vals = jnp.where(rows == idx, -jnp.inf, vals)
    return jnp.concatenate(top_v, axis=0), jnp.concatenate(top_i, axis=0)


def _topk_kernel(qry_ref, keys_ref, idx_ref, gate_ref):
    tb = qry_ref.shape[0]
    tops = []
    for p in range(2):
        q = qry_ref[:, p * N_KEYS:(p + 1) * N_KEYS]
        sc = lax.dot_general(keys_ref[0, p], q, (((1,), (1,)), ((), ())), preferred_element_type=F32)
        tops.append(_top_rows(sc, PK_TOPK))
    (s1, i1), (s2, i2) = tops
    cand = (s1[:, None, :] + s2[None, :, :]).reshape(PK_TOPK * PK_TOPK, tb)
    cidx = (i1[:, None, :] * float(N_KEYS) + i2[None, :, :]).reshape(PK_TOPK * PK_TOPK, tb)
    best, pos = _top_rows(cand, PK_TOPK)
    rows = lax.broadcasted_iota(I32, cand.shape, 0).astype(F32)
    picked = [jnp.max(jnp.where(rows == pos[j:j + 1], cidx, -1.0), axis=0, keepdims=True)
              for j in range(PK_TOPK)]
    idx_ref[...] = jnp.concatenate(picked, axis=0).astype(I32)
    e = jnp.exp(best - best[0:1])
    gate_ref[...] = e / jnp.sum(e, axis=0, keepdims=True)


def _topk(qry, keys, tb):
    t = qry.shape[0]
    return pl.pallas_call(
        _topk_kernel,
        grid=(t // tb, PK_HEADS),
        in_specs=[pl.BlockSpec((tb, PK_DIM), lambda i, h: (i, h)),
                  pl.BlockSpec((1, 2, N_KEYS, PK_DIM // 2), lambda i, h: (h, 0, 0, 0))],
        out_specs=[pl.BlockSpec((PK_TOPK, tb), lambda i, h: (h, i)),
                   pl.BlockSpec((PK_TOPK, tb), lambda i, h: (h, i))],
        out_shape=[jax.ShapeDtypeStruct((N_PICKS, t), I32), jax.ShapeDtypeStruct((N_PICKS, t), F32)],
        compiler_params=_cparams(("parallel", "parallel")),
        name="topk",
    )(qry, keys)


def _pack_table(w):
    bits = lax.bitcast_convert_type(w.astype(BF16), jnp.uint16).astype(jnp.uint32)
    packed = bits[:, :ROW_WORDS] | (bits[:, ROW_WORDS:] << 16)
    return lax.bitcast_convert_type(packed, I32).reshape(w.shape[0] * ROW_SUB, LANES)


def _unpack(words):
    lo = pltpu.bitcast(words << 16, F32)
    hi = pltpu.bitcast(words & jnp.int32(-65536), F32)
    return lo, hi


def _load_table(tab_hbm, tab_vmem, sem):
    @pl.when(pl.program_id(0) == 0)
    def _():
        cp = pltpu.make_async_copy(tab_hbm, tab_vmem, sem)
        cp.start()
        cp.wait()


def _peer_u_kernel(idx_ref, h_ref, gate_ref, tab_hbm, coef_ref, tab, tile, sem):
    _load_table(tab_hbm, tab, sem)
    tb = h_ref.shape[0]
    lane = lax.broadcasted_iota(I32, (N_PICKS, LANES), 1)
    for g in range(tb // LANES):
        def body(j, dacc):
            t = g * LANES + j
            for k in range(N_PICKS):
                r = pl.multiple_of(idx_ref[t, k] * ROW_SUB, ROW_SUB)
                tile[pl.ds(k, ROW_SUB, stride=TILE_STRIDE), :] = tab[pl.ds(r, ROW_SUB), :]
            xrow = h_ref[t]
            acc = jnp.zeros((N_PICKS, LANES), F32)
            for c in range(ROW_SUB):
                lo, hi = _unpack(tile[c * TILE_STRIDE:c * TILE_STRIDE + N_PICKS, :])
                acc = acc + lo * xrow[c:c + 1, :] + hi * xrow[ROW_SUB + c:ROW_SUB + c + 1, :]
            col = jnp.sum(acc, axis=1, keepdims=True)
            return jnp.where(lane == j, col, dacc)

        d = lax.fori_loop(0, LANES, body, jnp.zeros((N_PICKS, LANES), F32))
        sl = slice(g * LANES, (g + 1) * LANES)
        gelu = 0.5 * d * (1.0 + lax.erf(d * (0.5 ** 0.5)))
        coef_ref[:, sl] = gate_ref[:, sl] * gelu


def _peer_u(idx, h3, gate_t, tab, tb):
    t = idx.shape[0]
    return pl.pallas_call(
        _peer_u_kernel,
        grid=(t // tb,),
        in_specs=[pl.BlockSpec((tb, N_PICKS), lambda i: (i, 0), memory_space=pltpu.SMEM),
                  pl.BlockSpec((tb, SUBLANES, LANES), lambda i: (i, 0, 0)),
                  pl.BlockSpec((N_PICKS, tb), lambda i: (0, i)),
                  pl.BlockSpec(memory_space=pl.ANY)],
        out_specs=pl.BlockSpec((N_PICKS, tb), lambda i: (0, i)),
        out_shape=jax.ShapeDtypeStruct((N_PICKS, t), F32),
        scratch_shapes=[pltpu.VMEM(tab.shape, I32),
                        pltpu.VMEM((ROW_SUB * TILE_STRIDE, LANES), I32),
                        pltpu.SemaphoreType.DMA(())],
        compiler_params=_cparams(("arbitrary",)),
        name="peer_u",
    )(idx, h3, gate_t, tab)


def _peer_v_kernel(idx_ref, coef_ref, tab_hbm, out_ref, tab, sem):
    _load_table(tab_hbm, tab, sem)
    tb = out_ref.shape[0]
    nacc = 4

    def body(t, carry):
        lo_acc = [jnp.zeros((ROW_SUB, LANES), F32) for _ in range(nacc)]
        hi_acc = [jnp.zeros((ROW_SUB, LANES), F32) for _ in range(nacc)]
        for k in range(N_PICKS):
            r = pl.multiple_of(idx_ref[t, k] * ROW_SUB, ROW_SUB)
            lo, hi = _unpack(tab[pl.ds(r, ROW_SUB), :])
            cf = coef_ref[t, k]
            lo_acc[k % nacc] = lo_acc[k % nacc] + cf * lo
            hi_acc[k % nacc] = hi_acc[k % nacc] + cf * hi
        lo = (lo_acc[0] + lo_acc[1]) + (lo_acc[2] + lo_acc[3])
        hi = (hi_acc[0] + hi_acc[1]) + (hi_acc[2] + hi_acc[3])
        out_ref[t] = jnp.concatenate([lo, hi], axis=0)
        return carry

    lax.fori_loop(0, tb, body, 0)


def _peer_v(idx, coef, tab, tb):
    t = idx.shape[0]
    smem = lambda: pl.BlockSpec((tb, N_PICKS), lambda i: (i, 0), memory_space=pltpu.SMEM)
    return pl.pallas_call(
        _peer_v_kernel,
        grid=(t // tb,),
        in_specs=[smem(), smem(), pl.BlockSpec(memory_space=pl.ANY)],
        out_specs=pl.BlockSpec((tb, SUBLANES, LANES), lambda i: (i, 0, 0)),
        out_shape=jax.ShapeDtypeStruct((t, SUBLANES, LANES), F32),
        scratch_shapes=[pltpu.VMEM(tab.shape, I32), pltpu.SemaphoreType.DMA(())],
        compiler_params=_cparams(("arbitrary",)),
        name="peer_v",
    )(idx, coef, tab)


def _final_kernel(x_ref, p_ref, g_ref, y_ref):
    y_ref[...] = _rms(x_ref[...] + p_ref[...]) * g_ref[...]


def _final(x2, peer, g, tb):
    t = x2.shape[0]
    row = pl.BlockSpec((tb, D_MODEL), lambda i: (i, 0))
    return pl.pallas_call(
        _final_kernel,
        grid=(t // tb,),
        in_specs=[row, row, _const_spec((1, D_MODEL))],
        out_specs=row,
        out_shape=jax.ShapeDtypeStruct((t, D_MODEL), F32),
        compiler_params=_cparams(("parallel",)),
        name="final",
    )(x2, peer, g)


def _rope_tables(seq):
    def cos_sin(pos, dim):
        inv = ROPE_THETA ** (-jnp.arange(0, dim, 2, dtype=F32) / dim)
        ang = pos.astype(F32)[:, None] * inv[None, :]
        ang = jnp.concatenate([ang, ang], axis=-1)
        return jnp.cos(ang), jnp.sin(ang)

    sign = jnp.where(jnp.arange(64) < 32, -1.0, 1.0).astype(F32)
    ca, sa = cos_sin(jnp.arange(seq), A_HD)
    rows = seq // GRID_W
    cr, sr = cos_sin(jnp.repeat(jnp.arange(rows), GRID_W), B_HD // 2)
    cc, sc = cos_sin(jnp.tile(jnp.arange(GRID_W), rows), B_HD // 2)
    cos_a = jnp.concatenate([ca, ca], axis=-1)
    sin_a = jnp.concatenate([sa * sign, sa * sign], axis=-1)
    cos_b = jnp.concatenate([cr, cc], axis=-1)
    sin_b = jnp.concatenate([sr * sign, sc * sign], axis=-1)
    return jnp.stack([cos_a, sin_a, cos_b, sin_b])


def _trunk(x, w, tb_proj=256, tb_peer=128):
    b, s, d = x.shape
    t = b * s
    x2d = x.reshape(t, d)
    cs = _rope_tables(s)
    qa, ka, va, qb, kb, vb = _inproj(x2d, w["g_mix"], w["w_in"], cs, w["qn"], w["kn"], s, min(tb_proj, s))
    sh = lambda a: a.reshape(b, s, a.shape[-1])
    oa = _window(w["sink"], sh(qa), sh(ka), sh(va))
    ob = _dense(sh(qb), sh(kb), sh(vb))
    x2, h2, qry = _mixout(x2d, oa.reshape(t, A_Q), ob.reshape(t, B_Q), w["g_mix"], w["w_gate"], w["b_gate"],
                          w["w_oa"], w["w_ob"], w["w_out"], w["g_ffn"], w["w_query"], tb_proj)
    idx_t, gate_t = _topk(qry, w["keys"], tb_proj)
    idx = idx_t.T
    coef_t = _peer_u(idx, h2.reshape(t, SUBLANES, LANES), gate_t, w["tab_u"], tb_peer)
    peer = _peer_v(idx, coef_t.T, w["tab_v"], tb_peer)
    y = _final(x2, peer.reshape(t, d), w["g_final"], tb_proj)
    return y.reshape(b, s, d)


def kernel(x_prompt, x_sample, g_mix, w_in, w_gate, b_gate, sink, q_norm_g, k_norm_g, w_oa, w_ob, w_out,
           g_ffn, w_query, sub_keys, expert_u, expert_v, g_final):
    l = 0
    w = {
        "g_mix": g_mix[l][None, :], "w_in": w_in[l].astype(BF16),
        "w_gate": w_gate[l].astype(BF16), "b_gate": b_gate[l][None, :],
        "sink": sink[l], "qn": q_norm_g[l][None, :], "kn": k_norm_g[l][None, :],
        "w_oa": w_oa[l].astype(BF16), "w_ob": w_ob[l].astype(BF16), "w_out": w_out[l].astype(BF16),
        "g_ffn": g_ffn[l][None, :], "w_query": w_query[l].astype(BF16),
        "keys": sub_keys[l].astype(BF16),
        "tab_u": _pack_table(expert_u[l]), "tab_v": _pack_table(expert_v[l]),
        "g_final": g_final[None, :],
    }
    return _trunk(x_prompt, w), _trunk(x_sample, w)
```

```python
import functools

import jax
import jax.numpy as jnp
from jax import lax
from jax.experimental import pallas as pl
from jax.experimental.pallas import tpu as pltpu

F32 = jnp.float32
BF16 = jnp.bfloat16
I32 = jnp.int32

D_MODEL = 1024
EPS = 1e-6
NEG_INF = -1e30
ROPE_THETA = 10000.0
GRID_W = 64
WINDOW = 128
A_HEADS, A_KV, A_HD = 16, 4, 64
B_HEADS, B_KV, B_HD = 8, 2, 128
A_Q, A_KVW = A_HEADS * A_HD, A_KV * A_HD
B_Q, B_KVW = B_HEADS * B_HD, B_KV * B_HD
IN_COLS = A_Q + 2 * A_KVW + B_Q + 2 * B_KVW
PK_HEADS, N_KEYS, PK_TOPK = 8, 128, 16
PK_DIM = 256
N_PICKS = PK_HEADS * PK_TOPK
LANES = 128
SUBLANES = 8
ROW_WORDS = D_MODEL // 2
ROW_SUB = ROW_WORDS // LANES
TILE_STRIDE = N_PICKS + SUBLANES
TOKENS_PER_ITER = 8
LOG2E = 1.4426950408889634
VMEM_LIMIT = 56 * 1024 * 1024


def _cparams(sem):
    return pltpu.CompilerParams(dimension_semantics=sem, vmem_limit_bytes=VMEM_LIMIT)


def _const_spec(shape):
    nd = len(shape)
    return pl.BlockSpec(shape, lambda *_: (0,) * nd, pipeline_mode=pl.Buffered(1))


def _rms(x):
    return x * lax.rsqrt(jnp.mean(x * x, axis=-1, keepdims=True) + EPS)


def _inproj_kernel(x_ref, g_ref, w_ref, cs_ref, qn_ref, kn_ref,
                   qa_ref, ka_ref, va_ref, qb_ref, kb_ref, vb_ref):
    tb = x_ref.shape[0]
    hb = (_rms(x_ref[...]) * g_ref[...]).astype(BF16)
    lane = lax.broadcasted_iota(I32, (tb, LANES), 1)
    first = (lane % 64) < 32
    cos_a, sin_a, cos_b, sin_b = cs_ref[0], cs_ref[1], cs_ref[2], cs_ref[3]

    def rope(xc, cos, sin_signed):
        rot = jnp.where(first, pltpu.roll(xc, 96, 1), pltpu.roll(xc, 32, 1))
        return xc * cos + rot * sin_signed

    def proj(c0, width):
        return jnp.dot(hb, w_ref[:, c0:c0 + width], preferred_element_type=F32)

    for j in range(A_Q // 256):
        y = proj(j * 256, 256)
        for s in range(2):
            yc = rope(y[:, s * LANES:(s + 1) * LANES], cos_a, sin_a) * (A_HD ** -0.5)
            qa_ref[:, j * 256 + s * LANES:j * 256 + (s + 1) * LANES] = yc.astype(BF16)
    y = proj(A_Q, 256)
    for s in range(2):
        ka_ref[:, s * LANES:(s + 1) * LANES] = rope(y[:, s * LANES:(s + 1) * LANES], cos_a, sin_a).astype(BF16)
    va_ref[...] = proj(A_Q + A_KVW, 256).astype(BF16)
    base = A_Q + 2 * A_KVW
    for j in range(B_Q // 256):
        y = proj(base + j * 256, 256)
        for s in range(2):
            yc = _rms(y[:, s * LANES:(s + 1) * LANES]) * qn_ref[...]
            yc = rope(yc, cos_b, sin_b) * (B_HD ** -0.5 * LOG2E)
            qb_ref[:, j * 256 + s * LANES:j * 256 + (s + 1) * LANES] = yc.astype(BF16)
    y = proj(base + B_Q, 256)
    for s in range(2):
        yc = _rms(y[:, s * LANES:(s + 1) * LANES]) * kn_ref[...]
        kb_ref[:, s * LANES:(s + 1) * LANES] = rope(yc, cos_b, sin_b).astype(BF16)
    y = proj(base + B_Q + B_KVW, 256).astype(BF16)
    ones = jnp.ones((tb, B_HD), BF16)
    for kv in range(B_KV):
        vb_ref[:, 2 * kv * B_HD:(2 * kv + 1) * B_HD] = y[:, kv * B_HD:(kv + 1) * B_HD]
        vb_ref[:, (2 * kv + 1) * B_HD:(2 * kv + 2) * B_HD] = ones


def _inproj(x2d, g_mix, w_in, cs, qn, kn, seq, tb):
    t = x2d.shape[0]
    ns = seq // tb
    row = lambda w: pl.BlockSpec((tb, w), lambda i: (i, 0))
    out = lambda w: jax.ShapeDtypeStruct((t, w), BF16)
    return pl.pallas_call(
        _inproj_kernel,
        grid=(t // tb,),
        in_specs=[row(D_MODEL), _const_spec((1, D_MODEL)), _const_spec((D_MODEL, IN_COLS)),
                  pl.BlockSpec((4, tb, LANES), lambda i: (0, i % ns, 0)),
                  _const_spec((1, LANES)), _const_spec((1, LANES))],
        out_specs=[row(A_Q), row(A_KVW), row(A_KVW), row(B_Q), row(B_KVW), row(2 * B_KVW)],
        out_shape=[out(A_Q), out(A_KVW), out(A_KVW), out(B_Q), out(B_KVW), out(2 * B_KVW)],
        compiler_params=_cparams(("parallel",)),
        name="inproj",
    )(x2d, g_mix, w_in, cs, qn, kn)


def _window_kernel(sink_ref, q_ref, kp_ref, kc_ref, kn_ref, vp_ref, vc_ref, vn_ref, o_ref):
    i = pl.program_id(1)
    nb = pl.num_programs(1)
    blk = q_ref.shape[0]
    r = lax.broadcasted_iota(I32, (blk, 3 * blk), 0)
    c = lax.broadcasted_iota(I32, (blk, 3 * blk), 1)
    rel = c - blk - r
    lo_ok = jnp.where(i > 0, 0, blk)
    hi_ok = jnp.where(i < nb - 1, 3 * blk, 2 * blk)
    ok = (jnp.abs(rel) <= WINDOW) & (c >= lo_ok) & (c < hi_ok)
    k3 = jnp.concatenate([kp_ref[...], kc_ref[...], kn_ref[...]], axis=0)
    v3 = jnp.concatenate([vp_ref[...], vc_ref[...], vn_ref[...]], axis=0)
    group = A_HEADS // A_KV
    outs = []
    for h in range(A_HEADS):
        kv = h // group
        qh = q_ref[:, h * A_HD:(h + 1) * A_HD]
        kh = k3[:, kv * A_HD:(kv + 1) * A_HD]
        vh = v3[:, kv * A_HD:(kv + 1) * A_HD]
        s = lax.dot_general(qh, kh, (((1,), (1,)), ((), ())), preferred_element_type=F32)
        s = jnp.where(ok, s, NEG_INF)
        sk = sink_ref[h]
        m = jnp.maximum(jnp.max(s, axis=1, keepdims=True), sk)
        p = jnp.exp(s - m)
        den = jnp.sum(p, axis=1, keepdims=True) + jnp.exp(sk - m)
        o = jnp.dot(p.astype(BF16), vh, preferred_element_type=F32) / den
        outs.append(o.astype(BF16))
    o_ref[...] = jnp.concatenate(outs, axis=1)


def _window(sink, qa, ka, va, blk=128):
    b, s, _ = qa.shape
    nb = s // blk
    qspec = pl.BlockSpec((None, blk, A_Q), lambda bi, i: (bi, i, 0))
    prev = pl.BlockSpec((None, blk, A_KVW), lambda bi, i: (bi, jnp.maximum(i - 1, 0), 0))
    cur = pl.BlockSpec((None, blk, A_KVW), lambda bi, i: (bi, i, 0))
    nxt = pl.BlockSpec((None, blk, A_KVW), lambda bi, i: (bi, jnp.minimum(i + 1, nb - 1), 0))
    return pl.pallas_call(
        _window_kernel,
        grid=(b, nb),
        in_specs=[pl.BlockSpec(memory_space=pltpu.SMEM), qspec, prev, cur, nxt, prev, cur, nxt],
        out_specs=qspec,
        out_shape=jax.ShapeDtypeStruct((b, s, A_Q), BF16),
        compiler_params=_cparams(("parallel", "parallel")),
        name="window",
    )(sink, qa, ka, ka, ka, va, va, va)


def _dense_kernel(q_ref, k_ref, v_ref, o_ref, m_sc, acc_sc, *, tk, rb):
    tq = q_ref.shape[0]
    group = B_HEADS // B_KV
    qs = [q_ref[i * rb:(i + 1) * rb, g * B_HD:(g + 1) * B_HD] for g in range(group) for i in range(tq // rb)]
    m_sc[...] = jnp.full(m_sc.shape, -jnp.inf, F32)
    acc_sc[...] = jnp.zeros(acc_sc.shape, F32)

    def body(j, carry):
        off = pl.multiple_of(j * tk, tk)
        k = k_ref[pl.ds(off, tk), :]
        v = v_ref[pl.ds(off, tk), :]
        for n, q in enumerate(qs):
            rows = slice(n * rb, (n + 1) * rb)
            s = lax.dot_general(q, k, (((1,), (1,)), ((), ())), preferred_element_type=F32)
            m_prev = m_sc[rows]
            m_next = jnp.maximum(m_prev, jnp.max(s, axis=1, keepdims=True))
            p = jnp.exp2(s - jnp.concatenate([m_next] * (tk // LANES), axis=1))
            alpha = jnp.exp2(m_prev - m_next)
            acc_sc[rows] = jnp.concatenate([alpha, alpha], axis=1) * acc_sc[rows] + jnp.dot(
                p.astype(BF16), v, preferred_element_type=F32)
            m_sc[rows] = m_next
        return carry

    lax.fori_loop(0, k_ref.shape[0] // tk, body, 0)
    for g in range(group):
        acc = acc_sc[g * tq:(g + 1) * tq]
        o_ref[:, g * B_HD:(g + 1) * B_HD] = (acc[:, :B_HD] / acc[:, B_HD:]).astype(BF16)


def _dense(qb, kb, vb1, tq=512, tk=1024, rb=256):
    b, s, _ = qb.shape
    tq, tk = min(tq, s), min(tk, s)
    rb = min(rb, tq)
    group = B_HEADS // B_KV
    gw = group * B_HD
    qspec = pl.BlockSpec((None, tq, gw), lambda bi, kv, i: (bi, i, kv))
    kspec = pl.BlockSpec((None, s, B_HD), lambda bi, kv, i: (bi, 0, kv))
    vspec = pl.BlockSpec((None, s, 2 * B_HD), lambda bi, kv, i: (bi, 0, kv))
    return pl.pallas_call(
        functools.partial(_dense_kernel, tk=tk, rb=rb),
        grid=(b, B_KV, s // tq),
        in_specs=[qspec, kspec, vspec],
        out_specs=qspec,
        out_shape=jax.ShapeDtypeStruct((b, s, B_Q), BF16),
        scratch_shapes=[pltpu.VMEM((group * tq, B_HD), F32), pltpu.VMEM((group * tq, 2 * B_HD), F32)],
        compiler_params=_cparams(("parallel", "parallel", "parallel")),
        name="dense",
    )(qb, kb, vb1)


def _mixout_kernel(x_ref, oa_ref, ob_ref, gmix_ref, wg_ref, bg_ref, woa_ref, wob_ref, wout_ref,
                   gffn_ref, wq_ref, x2_ref, h2_ref, qry_ref):
    x = x_ref[...]
    hb = (_rms(x) * gmix_ref[...]).astype(BF16)
    gates = jax.nn.sigmoid(jnp.dot(hb, wg_ref[...], preferred_element_type=F32) + bg_ref[...])
    ya = jnp.dot(oa_ref[...], woa_ref[...], preferred_element_type=F32)
    yb = jnp.dot(ob_ref[...], wob_ref[...], preferred_element_type=F32)
    merged = (gates[:, :D_MODEL] * ya + gates[:, D_MODEL:] * yb).astype(BF16)
    x2 = x + jnp.dot(merged, wout_ref[...], preferred_element_type=F32)
    x2_ref[...] = x2
    h2 = _rms(x2) * gffn_ref[...]
    h2_ref[...] = h2
    qry_ref[...] = jnp.dot(h2.astype(BF16), wq_ref[...], preferred_element_type=F32).astype(BF16)


def _mixout(x2d, oa, ob, gmix, wg, bg, woa, wob, wout, gffn, wq, tb):
    t = x2d.shape[0]
    row = lambda w: pl.BlockSpec((tb, w), lambda i: (i, 0))
    qw = PK_HEADS * PK_DIM
    return pl.pallas_call(
        _mixout_kernel,
        grid=(t // tb,),
        in_specs=[row(D_MODEL), row(A_Q), row(B_Q), _const_spec((1, D_MODEL)),
                  _const_spec((D_MODEL, 2 * D_MODEL)), _const_spec((1, 2 * D_MODEL)),
                  _const_spec((A_Q, D_MODEL)), _const_spec((B_Q, D_MODEL)), _const_spec((D_MODEL, D_MODEL)),
                  _const_spec((1, D_MODEL)), _const_spec((D_MODEL, qw))],
        out_specs=[row(D_MODEL), row(D_MODEL), row(qw)],
        out_shape=[jax.ShapeDtypeStruct((t, D_MODEL), F32), jax.ShapeDtypeStruct((t, D_MODEL), F32),
                   jax.ShapeDtypeStruct((t, qw), BF16)],
        compiler_params=_cparams(("parallel",)),
        name="mixout",
    )(x2d, oa, ob, gmix, wg, bg, woa, wob, wout, gffn, wq)


def _top_rows(vals, ids, k):
    big = float(PK_TOPK * PK_TOPK)
    top_v, top_i = [], []
    for _ in range(k):
        m = jnp.max(vals, axis=0, keepdims=True)
        idx = jnp.min(jnp.where(vals == m, ids, big), axis=0, keepdims=True)
        top_v.append(m)
        top_i.append(idx)
        vals = jnp.where(ids == idx, -jnp.inf, vals)
    return jnp.concatenate(top_v, axis=0), jnp.concatenate(top_i, axis=0)


def _topk_kernel(qry_ref, keys_ref, idx_ref, gate_ref):
    tb = qry_ref.shape[0]
    row_id = lambda n: lax.broadcasted_iota(I32, (n, tb), 0).astype(F32)
    tops = []
    for p in range(2):
        q = qry_ref[:, p * N_KEYS:(p + 1) * N_KEYS]
        sc = lax.dot_general(keys_ref[0, p], q, (((1,), (1,)), ((), ())), preferred_element_type=F32)
        tops.append(_top_rows(sc, row_id(N_KEYS), PK_TOPK))
    (s1, i1), (s2, i2) = tops
    half = PK_TOPK // 2
    cand = [s1[0:1] + s2]
    cpos = [row_id(PK_TOPK)]
    cidx = [i1[0:1] * float(N_KEYS) + i2]
    for a in range(1, half):
        cand.append(s1[a:a + 1] + s2[:half])
        cpos.append(row_id(half) + float(a * PK_TOPK))
        cidx.append(i1[a:a + 1] * float(N_KEYS) + i2[:half])
    cand.append(s1[half:] + s2[0:1])
    cpos.append((row_id(half) + float(half)) * float(PK_TOPK))
    cidx.append(i1[half:] * float(N_KEYS) + i2[0:1])
    cand, cpos, cidx = (jnp.concatenate(x, axis=0) for x in (cand, cpos, cidx))
    best, pos = _top_rows(cand, cpos, PK_TOPK)
    picked = [jnp.max(jnp.where(cpos == pos[j:j + 1], cidx, -1.0), axis=0, keepdims=True)
              for j in range(PK_TOPK)]
    idx_ref[...] = jnp.concatenate(picked, axis=0).astype(I32)
    e = jnp.exp(best - best[0:1])
    gate_ref[...] = e / jnp.sum(e, axis=0, keepdims=True)


def _topk(qry, keys, tb):
    t = qry.shape[0]
    return pl.pallas_call(
        _topk_kernel,
        grid=(t // tb, PK_HEADS),
        in_specs=[pl.BlockSpec((tb, PK_DIM), lambda i, h: (i, h)),
                  pl.BlockSpec((1, 2, N_KEYS, PK_DIM // 2), lambda i, h: (h, 0, 0, 0))],
        out_specs=[pl.BlockSpec((PK_TOPK, tb), lambda i, h: (h, i)),
                   pl.BlockSpec((PK_TOPK, tb), lambda i, h: (h, i))],
        out_shape=[jax.ShapeDtypeStruct((N_PICKS, t), I32), jax.ShapeDtypeStruct((N_PICKS, t), F32)],
        compiler_params=_cparams(("parallel", "parallel")),
        name="topk",
    )(qry, keys)


def _pack_table(w):
    bits = lax.bitcast_convert_type(w.astype(BF16), jnp.uint16).astype(jnp.uint32)
    packed = bits[:, :ROW_WORDS] | (bits[:, ROW_WORDS:] << 16)
    return lax.bitcast_convert_type(packed, I32).reshape(w.shape[0] * ROW_SUB, LANES)


def _split3(x):
    hi = x.astype(BF16).astype(F32)
    r1 = x - hi
    mid = r1.astype(BF16).astype(F32)
    lo = (r1 - mid).astype(BF16).astype(F32)
    return hi, mid, lo


def _load_table(tab_hbm, tab_vmem, sem):
    @pl.when(pl.program_id(0) == 0)
    def _():
        cp = pltpu.make_async_copy(tab_hbm, tab_vmem, sem)
        cp.start()
        cp.wait()


def _gather(rows_ref, t, tab, tile):
    for k in range(N_PICKS):
        r = pl.multiple_of(rows_ref[t, k], ROW_SUB)
        tile[pl.ds(k, ROW_SUB, stride=TILE_STRIDE), :] = tab[pl.ds(r, ROW_SUB), :]


def _tile_chunk(tile, c):
    return pltpu.bitcast(tile[c * TILE_STRIDE:c * TILE_STRIDE + N_PICKS, :], BF16)


def _token_loop(tb, tiles, token):
    def body(i, carry):
        for u, tile in enumerate(tiles):
            token(len(tiles) * i + u, tile)
        return carry

    lax.fori_loop(0, tb // len(tiles), body, 0)


def _peer_u_kernel(rows_ref, h_ref, gate_ref, tab_hbm, coef_ref, tab, zbuf, sem, *tiles):
    _load_table(tab_hbm, tab, sem)
    sub = lax.broadcasted_iota(I32, (SUBLANES, 2 * N_PICKS), 0)
    lane = lax.broadcasted_iota(I32, (SUBLANES, 2 * N_PICKS), 1)
    keep = (sub < ROW_SUB) == ((lane % 2) == 0)
    zeros = jnp.zeros((SUBLANES, LANES), F32)

    def token(t, tile):
        _gather(rows_ref, t, tab, tile)
        hi, mid, lo = _split3(h_ref[t])
        lhs = jnp.concatenate([hi, mid, lo, zeros], axis=0).astype(BF16)
        z = jnp.zeros((SUBLANES, 2 * N_PICKS), F32)
        for c in range(ROW_SUB):
            r = lax.dot_general(lhs, _tile_chunk(tile, c), (((1,), (1,)), ((), ())),
                                preferred_element_type=F32)
            rc = r[0:8] + r[8:16] + r[16:24]
            z = jnp.where((sub % ROW_SUB) == c, rc, z)
        zbuf[pl.ds(t, 1), :] = jnp.sum(jnp.where(keep, z, 0.0), axis=0, keepdims=True)

    _token_loop(h_ref.shape[0], tiles, token)
    zs = zbuf[...]
    d = zs + pltpu.roll(zs, 2 * N_PICKS - 1, 1)
    gelu = 0.5 * d * (1.0 + lax.erf(d * (0.5 ** 0.5)))
    coef_ref[...] = gate_ref[...] * gelu


def _tile_scratch():
    return [pltpu.VMEM((ROW_SUB * TILE_STRIDE, LANES), I32) for _ in range(TOKENS_PER_ITER)]


def _peer_u(rows, h3, gate2, tab, tb):
    t = rows.shape[0]
    return pl.pallas_call(
        _peer_u_kernel,
        grid=(t // tb,),
        in_specs=[pl.BlockSpec((tb, N_PICKS), lambda i: (i, 0), memory_space=pltpu.SMEM),
                  pl.BlockSpec((tb, SUBLANES, LANES), lambda i: (i, 0, 0)),
                  pl.BlockSpec((tb, 2 * N_PICKS), lambda i: (i, 0)),
                  pl.BlockSpec(memory_space=pl.ANY)],
        out_specs=pl.BlockSpec((tb, 2 * N_PICKS), lambda i: (i, 0)),
        out_shape=jax.ShapeDtypeStruct((t, 2 * N_PICKS), F32),
        scratch_shapes=[pltpu.VMEM(tab.shape, I32), pltpu.VMEM((tb, 2 * N_PICKS), F32),
                        pltpu.SemaphoreType.DMA(())] + _tile_scratch(),
        compiler_params=_cparams(("arbitrary",)),
        name="peer_u",
    )(rows, h3, gate2, tab)


def _peer_v_kernel(rows_ref, lhs_ref, tab_hbm, out_ref, tab, sem, *tiles):
    _load_table(tab_hbm, tab, sem)
    sub = lax.broadcasted_iota(I32, (SUBLANES, LANES), 0)

    def token(t, tile):
        _gather(rows_ref, t, tab, tile)
        lhs = lhs_ref[t].astype(BF16)
        out = jnp.zeros((SUBLANES, LANES), F32)
        for c in range(ROW_SUB):
            r = jnp.dot(lhs, _tile_chunk(tile, c), preferred_element_type=F32)
            ev = jnp.sum(jnp.where(sub % 2 == 0, r, 0.0), axis=0, keepdims=True)
            od = jnp.sum(jnp.where(sub % 2 == 1, r, 0.0), axis=0, keepdims=True)
            out = jnp.where(sub == c, ev, out)
            out = jnp.where(sub == ROW_SUB + c, od, out)
        out_ref[t] = out

    _token_loop(out_ref.shape[0], tiles, token)


def _peer_v(rows, lhs, tab, tb):
    t = rows.shape[0]
    return pl.pallas_call(
        _peer_v_kernel,
        grid=(t // tb,),
        in_specs=[pl.BlockSpec((tb, N_PICKS), lambda i: (i, 0), memory_space=pltpu.SMEM),
                  pl.BlockSpec((tb, SUBLANES, 2 * N_PICKS), lambda i: (i, 0, 0)),
                  pl.BlockSpec(memory_space=pl.ANY)],
        out_specs=pl.BlockSpec((tb, SUBLANES, LANES), lambda i: (i, 0, 0)),
        out_shape=jax.ShapeDtypeStruct((t, SUBLANES, LANES), F32),
        scratch_shapes=[pltpu.VMEM(tab.shape, I32), pltpu.SemaphoreType.DMA(())] + _tile_scratch(),
        compiler_params=_cparams(("arbitrary",)),
        name="peer_v",
    )(rows, lhs, tab)


def _coef_lhs(coef2):
    parts = zip(_split3(coef2), _split3(jnp.roll(coef2, 1, axis=1)))
    rows = [r for pair in parts for r in pair] + [jnp.zeros_like(coef2)] * 2
    return jnp.stack(rows, axis=1)


def _peer(idx_t, gate_t, h2, tab_u, tab_v, tb):
    t = h2.shape[0]
    rows = idx_t.T * ROW_SUB
    gate = gate_t.T
    gate2 = jnp.stack([gate, jnp.zeros_like(gate)], axis=-1).reshape(t, 2 * N_PICKS)
    coef2 = _peer_u(rows, h2.reshape(t, SUBLANES, LANES), gate2, tab_u, tb)
    out = _peer_v(rows, _coef_lhs(coef2), tab_v, tb)
    return out.reshape(t, D_MODEL)


def _final_kernel(x_ref, p_ref, g_ref, y_ref):
    y_ref[...] = _rms(x_ref[...] + p_ref[...]) * g_ref[...]


def _final(x2, peer, g, tb):
    t = x2.shape[0]
    row = pl.BlockSpec((tb, D_MODEL), lambda i: (i, 0))
    return pl.pallas_call(
        _final_kernel,
        grid=(t // tb,),
        in_specs=[row, row, _const_spec((1, D_MODEL))],
        out_specs=row,
        out_shape=jax.ShapeDtypeStruct((t, D_MODEL), F32),
        compiler_params=_cparams(("parallel",)),
        name="final",
    )(x2, peer, g)


def _rope_tables(seq):
    def cos_sin(pos, dim):
        inv = ROPE_THETA ** (-jnp.arange(0, dim, 2, dtype=F32) / dim)
        ang = pos.astype(F32)[:, None] * inv[None, :]
        ang = jnp.concatenate([ang, ang], axis=-1)
        return jnp.cos(ang), jnp.sin(ang)

    sign = jnp.where(jnp.arange(64) < 32, -1.0, 1.0).astype(F32)
    ca, sa = cos_sin(jnp.arange(seq), A_HD)
    rows = seq // GRID_W
    cr, sr = cos_sin(jnp.repeat(jnp.arange(rows), GRID_W), B_HD // 2)
    cc, sc = cos_sin(jnp.tile(jnp.arange(GRID_W), rows), B_HD // 2)
    cos_a = jnp.concatenate([ca, ca], axis=-1)
    sin_a = jnp.concatenate([sa * sign, sa * sign], axis=-1)
    cos_b = jnp.concatenate([cr, cc], axis=-1)
    sin_b = jnp.concatenate([sr * sign, sc * sign], axis=-1)
    return jnp.stack([cos_a, sin_a, cos_b, sin_b])


def _trunk(x, w, tb_proj=256, tb_peer=128):
    b, s, d = x.shape
    t = b * s
    x2d = x.reshape(t, d)
    cs = _rope_tables(s)
    qa, ka, va, qb, kb, vb1 = _inproj(x2d, w["g_mix"], w["w_in"], cs, w["qn"], w["kn"], s, min(tb_proj, s))
    sh = lambda a: a.reshape(b, s, a.shape[-1])
    oa = _window(w["sink"], sh(qa), sh(ka), sh(va))
    ob = _dense(sh(qb), sh(kb), sh(vb1))
    x2, h2, qry = _mixout(x2d, oa.reshape(t, A_Q), ob.reshape(t, B_Q), w["g_mix"], w["w_gate"], w["b_gate"],
                          w["w_oa"], w["w_ob"], w["w_out"], w["g_ffn"], w["w_query"], tb_proj)
    idx_t, gate_t = _topk(qry, w["keys"], tb_proj)
    peer = _peer(idx_t, gate_t, h2, w["tab_u"], w["tab_v"], tb_peer)
    y = _final(x2, peer, w["g_final"], tb_proj)
    return y.reshape(b, s, d)


def kernel(x_prompt, x_sample, g_mix, w_in, w_gate, b_gate, sink, q_norm_g, k_norm_g, w_oa, w_ob, w_out,
           g_ffn, w_query, sub_keys, expert_u, expert_v, g_final):
    assert w_in.shape[0] == 1, "single-layer trunk"
    l = 0
    w = {
        "g_mix": g_mix[l][None, :], "w_in": w_in[l].astype(BF16),
        "w_gate": w_gate[l].astype(BF16), "b_gate": b_gate[l][None, :],
        "sink": sink[l], "qn": q_norm_g[l][None, :], "kn": k_norm_g[l][None, :],
        "w_oa": w_oa[l].astype(BF16), "w_ob": w_ob[l].astype(BF16), "w_out": w_out[l].astype(BF16),
        "g_ffn": g_ffn[l][None, :], "w_query": w_query[l].astype(BF16),
        "keys": sub_keys[l].astype(BF16),
        "tab_u": _pack_table(expert_u[l]), "tab_v": _pack_table(expert_v[l]),
        "g_final": g_final[None, :],
    }
    return _trunk(x_prompt, w), _trunk(x_sample, w)
```

```python
import functools

import jax
import jax.numpy as jnp
from jax import lax
from jax.experimental import pallas as pl
from jax.experimental.pallas import tpu as pltpu

F32 = jnp.float32
BF16 = jnp.bfloat16
I32 = jnp.int32

D_MODEL = 1024
EPS = 1e-6
NEG_INF = -1e30
ROPE_THETA = 10000.0
GRID_W = 64
WINDOW = 128
A_HEADS, A_KV, A_HD = 16, 4, 64
B_HEADS, B_KV, B_HD = 8, 2, 128
A_Q, A_KVW = A_HEADS * A_HD, A_KV * A_HD
B_Q, B_KVW = B_HEADS * B_HD, B_KV * B_HD
IN_COLS = A_Q + 2 * A_KVW + B_Q + 2 * B_KVW
PK_HEADS, N_KEYS, PK_TOPK = 8, 128, 16
PK_DIM = 256
N_PICKS = PK_HEADS * PK_TOPK
LANES = 128
SUBLANES = 8
ROW_WORDS = D_MODEL // 2
ROW_SUB = ROW_WORDS // LANES
PICK_LANES = SUBLANES * N_PICKS
TOKENS_PER_ITER = 8
LOG2E = 1.4426950408889634
VMEM_LIMIT = 56 * 1024 * 1024


def _cparams(sem):
    return pltpu.CompilerParams(dimension_semantics=sem, vmem_limit_bytes=VMEM_LIMIT)


def _const_spec(shape):
    nd = len(shape)
    return pl.BlockSpec(shape, lambda *_: (0,) * nd, pipeline_mode=pl.Buffered(1))


def _rms(x):
    return x * lax.rsqrt(jnp.mean(x * x, axis=-1, keepdims=True) + EPS)


def _inproj_kernel(x_ref, g_ref, w_ref, cs_ref, qn_ref, kn_ref,
                   qa_ref, ka_ref, va_ref, qb_ref, kb_ref, vb_ref):
    tb = x_ref.shape[0]
    hb = (_rms(x_ref[...]) * g_ref[...]).astype(BF16)
    lane = lax.broadcasted_iota(I32, (tb, LANES), 1)
    first = (lane % 64) < 32
    cos_a, sin_a, cos_b, sin_b = cs_ref[0], cs_ref[1], cs_ref[2], cs_ref[3]

    def rope(xc, cos, sin_signed):
        rot = jnp.where(first, pltpu.roll(xc, 96, 1), pltpu.roll(xc, 32, 1))
        return xc * cos + rot * sin_signed

    def proj(c0, width):
        return jnp.dot(hb, w_ref[:, c0:c0 + width], preferred_element_type=F32)

    for j in range(A_Q // 256):
        y = proj(j * 256, 256)
        for s in range(2):
            yc = rope(y[:, s * LANES:(s + 1) * LANES], cos_a, sin_a) * (A_HD ** -0.5)
            qa_ref[:, j * 256 + s * LANES:j * 256 + (s + 1) * LANES] = yc.astype(BF16)
    y = proj(A_Q, 256)
    for s in range(2):
        ka_ref[:, s * LANES:(s + 1) * LANES] = rope(y[:, s * LANES:(s + 1) * LANES], cos_a, sin_a).astype(BF16)
    va_ref[...] = proj(A_Q + A_KVW, 256).astype(BF16)
    base = A_Q + 2 * A_KVW
    for j in range(B_Q // 256):
        y = proj(base + j * 256, 256)
        for s in range(2):
            yc = _rms(y[:, s * LANES:(s + 1) * LANES]) * qn_ref[...]
            yc = rope(yc, cos_b, sin_b) * (B_HD ** -0.5 * LOG2E)
            qb_ref[:, j * 256 + s * LANES:j * 256 + (s + 1) * LANES] = yc.astype(BF16)
    y = proj(base + B_Q, 256)
    for s in range(2):
        yc = _rms(y[:, s * LANES:(s + 1) * LANES]) * kn_ref[...]
        kb_ref[:, s * LANES:(s + 1) * LANES] = rope(yc, cos_b, sin_b).astype(BF16)
    y = proj(base + B_Q + B_KVW, 256).astype(BF16)
    ones = jnp.ones((tb, B_HD), BF16)
    for kv in range(B_KV):
        vb_ref[:, 2 * kv * B_HD:(2 * kv + 1) * B_HD] = y[:, kv * B_HD:(kv + 1) * B_HD]
        vb_ref[:, (2 * kv + 1) * B_HD:(2 * kv + 2) * B_HD] = ones


def _inproj(x2d, g_mix, w_in, cs, qn, kn, seq, tb):
    t = x2d.shape[0]
    ns = seq // tb
    row = lambda w: pl.BlockSpec((tb, w), lambda i: (i, 0))
    out = lambda w: jax.ShapeDtypeStruct((t, w), BF16)
    return pl.pallas_call(
        _inproj_kernel,
        grid=(t // tb,),
        in_specs=[row(D_MODEL), _const_spec((1, D_MODEL)), _const_spec((D_MODEL, IN_COLS)),
                  pl.BlockSpec((4, tb, LANES), lambda i: (0, i % ns, 0)),
                  _const_spec((1, LANES)), _const_spec((1, LANES))],
        out_specs=[row(A_Q), row(A_KVW), row(A_KVW), row(B_Q), row(B_KVW), row(2 * B_KVW)],
        out_shape=[out(A_Q), out(A_KVW), out(A_KVW), out(B_Q), out(B_KVW), out(2 * B_KVW)],
        compiler_params=_cparams(("parallel",)),
        name="inproj",
    )(x2d, g_mix, w_in, cs, qn, kn)


def _window_kernel(sink_ref, q_ref, kp_ref, kc_ref, kn_ref, vp_ref, vc_ref, vn_ref, o_ref):
    i = pl.program_id(1)
    nb = pl.num_programs(1)
    blk = q_ref.shape[0]
    r = lax.broadcasted_iota(I32, (blk, 3 * blk), 0)
    c = lax.broadcasted_iota(I32, (blk, 3 * blk), 1)
    rel = c - blk - r
    lo_ok = jnp.where(i > 0, 0, blk)
    hi_ok = jnp.where(i < nb - 1, 3 * blk, 2 * blk)
    ok = (jnp.abs(rel) <= WINDOW) & (c >= lo_ok) & (c < hi_ok)
    k3 = jnp.concatenate([kp_ref[...], kc_ref[...], kn_ref[...]], axis=0)
    v3 = jnp.concatenate([vp_ref[...], vc_ref[...], vn_ref[...]], axis=0)
    group = A_HEADS // A_KV
    outs = []
    for h in range(A_HEADS):
        kv = h // group
        qh = q_ref[:, h * A_HD:(h + 1) * A_HD]
        kh = k3[:, kv * A_HD:(kv + 1) * A_HD]
        vh = v3[:, kv * A_HD:(kv + 1) * A_HD]
        s = lax.dot_general(qh, kh, (((1,), (1,)), ((), ())), preferred_element_type=F32)
        s = jnp.where(ok, s, NEG_INF)
        sk = sink_ref[h]
        m = jnp.maximum(jnp.max(s, axis=1, keepdims=True), sk)
        p = jnp.exp(s - m)
        den = jnp.sum(p, axis=1, keepdims=True) + jnp.exp(sk - m)
        o = jnp.dot(p.astype(BF16), vh, preferred_element_type=F32) / den
        outs.append(o.astype(BF16))
    o_ref[...] = jnp.concatenate(outs, axis=1)


def _window(sink, qa, ka, va, blk=128):
    b, s, _ = qa.shape
    nb = s // blk
    qspec = pl.BlockSpec((None, blk, A_Q), lambda bi, i: (bi, i, 0))
    prev = pl.BlockSpec((None, blk, A_KVW), lambda bi, i: (bi, jnp.maximum(i - 1, 0), 0))
    cur = pl.BlockSpec((None, blk, A_KVW), lambda bi, i: (bi, i, 0))
    nxt = pl.BlockSpec((None, blk, A_KVW), lambda bi, i: (bi, jnp.minimum(i + 1, nb - 1), 0))
    return pl.pallas_call(
        _window_kernel,
        grid=(b, nb),
        in_specs=[pl.BlockSpec(memory_space=pltpu.SMEM), qspec, prev, cur, nxt, prev, cur, nxt],
        out_specs=qspec,
        out_shape=jax.ShapeDtypeStruct((b, s, A_Q), BF16),
        compiler_params=_cparams(("parallel", "parallel")),
        name="window",
    )(sink, qa, ka, ka, ka, va, va, va)


def _dense_kernel(q_ref, k_ref, v_ref, o_ref, m_sc, acc_sc, *, tk, rb):
    tq = q_ref.shape[0]
    group = B_HEADS // B_KV
    qs = [q_ref[i * rb:(i + 1) * rb, g * B_HD:(g + 1) * B_HD] for g in range(group) for i in range(tq // rb)]
    m_sc[...] = jnp.full(m_sc.shape, -jnp.inf, F32)
    acc_sc[...] = jnp.zeros(acc_sc.shape, F32)

    def body(j, carry):
        off = pl.multiple_of(j * tk, tk)
        k = k_ref[pl.ds(off, tk), :]
        v = v_ref[pl.ds(off, tk), :]
        for n, q in enumerate(qs):
            rows = slice(n * rb, (n + 1) * rb)
            s = lax.dot_general(q, k, (((1,), (1,)), ((), ())), preferred_element_type=F32)
            m_prev = m_sc[rows]
            m_next = jnp.maximum(m_prev, jnp.max(s, axis=1, keepdims=True))
            p = jnp.exp2(s - jnp.concatenate([m_next] * (tk // LANES), axis=1))
            alpha = jnp.exp2(m_prev - m_next)
            acc_sc[rows] = jnp.concatenate([alpha, alpha], axis=1) * acc_sc[rows] + jnp.dot(
                p.astype(BF16), v, preferred_element_type=F32)
            m_sc[rows] = m_next
        return carry

    lax.fori_loop(0, k_ref.shape[0] // tk, body, 0)
    for g in range(group):
        acc = acc_sc[g * tq:(g + 1) * tq]
        o_ref[:, g * B_HD:(g + 1) * B_HD] = (acc[:, :B_HD] / acc[:, B_HD:]).astype(BF16)


def _dense(qb, kb, vb1, tq=512, tk=1024, rb=256):
    b, s, _ = qb.shape
    tq, tk = min(tq, s), min(tk, s)
    rb = min(rb, tq)
    group = B_HEADS // B_KV
    gw = group * B_HD
    qspec = pl.BlockSpec((None, tq, gw), lambda bi, kv, i: (bi, i, kv))
    kspec = pl.BlockSpec((None, s, B_HD), lambda bi, kv, i: (bi, 0, kv))
    vspec = pl.BlockSpec((None, s, 2 * B_HD), lambda bi, kv, i: (bi, 0, kv))
    return pl.pallas_call(
        functools.partial(_dense_kernel, tk=tk, rb=rb),
        grid=(b, B_KV, s // tq),
        in_specs=[qspec, kspec, vspec],
        out_specs=qspec,
        out_shape=jax.ShapeDtypeStruct((b, s, B_Q), BF16),
        scratch_shapes=[pltpu.VMEM((group * tq, B_HD), F32), pltpu.VMEM((group * tq, 2 * B_HD), F32)],
        compiler_params=_cparams(("parallel", "parallel", "parallel")),
        name="dense",
    )(qb, kb, vb1)


def _mixout_kernel(x_ref, oa_ref, ob_ref, gmix_ref, wg_ref, bg_ref, woa_ref, wob_ref, wout_ref,
                   gffn_ref, wq_ref, x2_ref, h2_ref, qry_ref):
    x = x_ref[...]
    hb = (_rms(x) * gmix_ref[...]).astype(BF16)
    gates = jax.nn.sigmoid(jnp.dot(hb, wg_ref[...], preferred_element_type=F32) + bg_ref[...])
    ya = jnp.dot(oa_ref[...], woa_ref[...], preferred_element_type=F32)
    yb = jnp.dot(ob_ref[...], wob_ref[...], preferred_element_type=F32)
    merged = (gates[:, :D_MODEL] * ya + gates[:, D_MODEL:] * yb).astype(BF16)
    x2 = x + jnp.dot(merged, wout_ref[...], preferred_element_type=F32)
    x2_ref[...] = x2
    h2 = _rms(x2) * gffn_ref[...]
    h2_ref[...] = h2
    qry_ref[...] = jnp.dot(h2.astype(BF16), wq_ref[...], preferred_element_type=F32).astype(BF16)


def _mixout(x2d, oa, ob, gmix, wg, bg, woa, wob, wout, gffn, wq, tb):
    t = x2d.shape[0]
    row = lambda w: pl.BlockSpec((tb, w), lambda i: (i, 0))
    qw = PK_HEADS * PK_DIM
    return pl.pallas_call(
        _mixout_kernel,
        grid=(t // tb,),
        in_specs=[row(D_MODEL), row(A_Q), row(B_Q), _const_spec((1, D_MODEL)),
                  _const_spec((D_MODEL, 2 * D_MODEL)), _const_spec((1, 2 * D_MODEL)),
                  _const_spec((A_Q, D_MODEL)), _const_spec((B_Q, D_MODEL)), _const_spec((D_MODEL, D_MODEL)),
                  _const_spec((1, D_MODEL)), _const_spec((D_MODEL, qw))],
        out_specs=[row(D_MODEL), row(D_MODEL), row(qw)],
        out_shape=[jax.ShapeDtypeStruct((t, D_MODEL), F32), jax.ShapeDtypeStruct((t, D_MODEL), F32),
                   jax.ShapeDtypeStruct((t, qw), BF16)],
        compiler_params=_cparams(("parallel",)),
        name="mixout",
    )(x2d, oa, ob, gmix, wg, bg, woa, wob, wout, gffn, wq)


def _top_rows(vals, ids, k):
    big = float(PK_TOPK * PK_TOPK)
    top_v, top_i = [], []
    for _ in range(k):
        m = jnp.max(vals, axis=0, keepdims=True)
        idx = jnp.min(jnp.where(vals == m, ids, big), axis=0, keepdims=True)
        top_v.append(m)
        top_i.append(idx)
        vals = jnp.where(ids == idx, -jnp.inf, vals)
    return jnp.concatenate(top_v, axis=0), jnp.concatenate(top_i, axis=0)


def _topk_kernel(qry_ref, keys_ref, idx_ref, gate_ref):
    tb = qry_ref.shape[0]
    row_id = lambda n: lax.broadcasted_iota(I32, (n, tb), 0).astype(F32)
    tops = []
    for p in range(2):
        q = qry_ref[:, p * N_KEYS:(p + 1) * N_KEYS]
        sc = lax.dot_general(keys_ref[0, p], q, (((1,), (1,)), ((), ())), preferred_element_type=F32)
        tops.append(_top_rows(sc, row_id(N_KEYS), PK_TOPK))
    (s1, i1), (s2, i2) = tops
    half = PK_TOPK // 2
    cand = [s1[0:1] + s2]
    cpos = [row_id(PK_TOPK)]
    cidx = [i1[0:1] * float(N_KEYS) + i2]
    for a in range(1, half):
        cand.append(s1[a:a + 1] + s2[:half])
        cpos.append(row_id(half) + float(a * PK_TOPK))
        cidx.append(i1[a:a + 1] * float(N_KEYS) + i2[:half])
    cand.append(s1[half:] + s2[0:1])
    cpos.append((row_id(half) + float(half)) * float(PK_TOPK))
    cidx.append(i1[half:] * float(N_KEYS) + i2[0:1])
    cand, cpos, cidx = (jnp.concatenate(x, axis=0) for x in (cand, cpos, cidx))
    best, pos = _top_rows(cand, cpos, PK_TOPK)
    picked = [jnp.max(jnp.where(cpos == pos[j:j + 1], cidx, -1.0), axis=0, keepdims=True)
              for j in range(PK_TOPK)]
    idx_ref[...] = jnp.concatenate(picked, axis=0).astype(I32)
    e = jnp.exp(best - best[0:1])
    gate_ref[...] = e / jnp.sum(e, axis=0, keepdims=True)


def _topk(qry, keys, tb):
    t = qry.shape[0]
    return pl.pallas_call(
        _topk_kernel,
        grid=(t // tb, PK_HEADS),
        in_specs=[pl.BlockSpec((tb, PK_DIM), lambda i, h: (i, h)),
                  pl.BlockSpec((1, 2, N_KEYS, PK_DIM // 2), lambda i, h: (h, 0, 0, 0))],
        out_specs=[pl.BlockSpec((PK_TOPK, tb), lambda i, h: (h, i)),
                   pl.BlockSpec((PK_TOPK, tb), lambda i, h: (h, i))],
        out_shape=[jax.ShapeDtypeStruct((N_PICKS, t), I32), jax.ShapeDtypeStruct((N_PICKS, t), F32)],
        compiler_params=_cparams(("parallel", "parallel")),
        name="topk",
    )(qry, keys)


def _pack_table(w):
    bits = lax.bitcast_convert_type(w.astype(BF16), jnp.uint16).astype(jnp.uint32)
    bits = bits.reshape(w.shape[0], ROW_SUB, 2, LANES)
    packed = bits[:, :, 0] | (bits[:, :, 1] << 16)
    return lax.bitcast_convert_type(packed, I32).reshape(w.shape[0] * ROW_SUB, LANES)


def _split3(x):
    hi = x.astype(BF16).astype(F32)
    r1 = x - hi
    mid = r1.astype(BF16).astype(F32)
    lo = (r1 - mid).astype(BF16).astype(F32)
    return hi, mid, lo


def _load_table(tab_hbm, tab_vmem, sem):
    @pl.when(pl.program_id(0) == 0)
    def _():
        cp = pltpu.make_async_copy(tab_hbm, tab_vmem, sem)
        cp.start()
        cp.wait()


def _weights(rows_ref, t, tab):
    rows = [tab[pl.ds(pl.multiple_of(rows_ref[t, k], ROW_SUB), ROW_SUB), :] for k in range(N_PICKS)]
    return pltpu.bitcast(jnp.concatenate(rows, axis=0), BF16)


def _token_loop(tb, token):
    def body(i, carry):
        for u in range(TOKENS_PER_ITER):
            token(TOKENS_PER_ITER * i + u)
        return carry

    lax.fori_loop(0, tb // TOKENS_PER_ITER, body, 0)


def _lhs3(x):
    hi, mid, lo = _split3(x)
    return jnp.concatenate([hi, mid, lo, jnp.zeros_like(x)], axis=0).astype(BF16)


def _diag_mask():
    shape = (SUBLANES, PICK_LANES)
    return (lax.broadcasted_iota(I32, shape, 1) % SUBLANES) == lax.broadcasted_iota(I32, shape, 0)


def _group_allsum(x):
    n = x.shape[1]
    lane = lax.broadcasted_iota(I32, x.shape, 1)
    dist = 1
    while dist < SUBLANES:
        x = x + jnp.where((lane & dist) == 0, pltpu.roll(x, n - dist, 1), pltpu.roll(x, dist, 1))
        dist *= 2
    return x


def _peer_u_kernel(rows_ref, h_ref, gate_ref, tab_hbm, c_ref, tab, dbuf, sem):
    _load_table(tab_hbm, tab, sem)
    diag = _diag_mask()

    def token(t):
        w = _weights(rows_ref, t, tab)
        r = lax.dot_general(_lhs3(h_ref[t]), w, (((1,), (1,)), ((), ())), preferred_element_type=F32)
        rc = r[0:8] + r[8:16] + r[16:24]
        dbuf[pl.ds(t, 1), :] = jnp.sum(jnp.where(diag, rc, 0.0), axis=0, keepdims=True)

    _token_loop(h_ref.shape[0], token)
    d = _group_allsum(dbuf[...])
    shape = (N_PICKS, PICK_LANES)
    spread = (lax.broadcasted_iota(I32, shape, 1) // SUBLANES == lax.broadcasted_iota(I32, shape, 0)).astype(BF16)
    g8 = sum(jnp.dot(p.astype(BF16), spread, preferred_element_type=F32) for p in _split3(gate_ref[...]))
    c_ref[...] = g8 * (0.5 * d * (1.0 + lax.erf(d * (0.5 ** 0.5))))


def _peer_u(rows, h3, gate, tab, tb):
    t = rows.shape[0]
    return pl.pallas_call(
        _peer_u_kernel,
        grid=(t // tb,),
        in_specs=[pl.BlockSpec((tb, N_PICKS), lambda i: (i, 0), memory_space=pltpu.SMEM),
                  pl.BlockSpec((tb, SUBLANES, LANES), lambda i: (i, 0, 0)),
                  pl.BlockSpec((tb, N_PICKS), lambda i: (i, 0)),
                  pl.BlockSpec(memory_space=pl.ANY)],
        out_specs=pl.BlockSpec((tb, PICK_LANES), lambda i: (i, 0)),
        out_shape=jax.ShapeDtypeStruct((t, PICK_LANES), F32),
        scratch_shapes=[pltpu.VMEM(tab.shape, I32), pltpu.VMEM((tb, PICK_LANES), F32),
                        pltpu.SemaphoreType.DMA(())],
        compiler_params=_cparams(("arbitrary",)),
        name="peer_u",
    )(rows, h3, gate, tab)


def _peer_v_kernel(rows_ref, c_ref, tab_hbm, out_ref, tab, sem):
    _load_table(tab_hbm, tab, sem)
    diag = _diag_mask()

    def token(t):
        w = _weights(rows_ref, t, tab)
        c8 = jnp.where(diag, c_ref[pl.ds(t, 1), :], 0.0)
        r = jnp.dot(_lhs3(c8), w, preferred_element_type=F32)
        out_ref[t] = r[0:8] + r[8:16] + r[16:24]

    _token_loop(out_ref.shape[0], token)


def _peer_v(rows, c8, tab, tb):
    t = rows.shape[0]
    return pl.pallas_call(
        _peer_v_kernel,
        grid=(t // tb,),
        in_specs=[pl.BlockSpec((tb, N_PICKS), lambda i: (i, 0), memory_space=pltpu.SMEM),
                  pl.BlockSpec((tb, PICK_LANES), lambda i: (i, 0)),
                  pl.BlockSpec(memory_space=pl.ANY)],
        out_specs=pl.BlockSpec((tb, SUBLANES, LANES), lambda i: (i, 0, 0)),
        out_shape=jax.ShapeDtypeStruct((t, SUBLANES, LANES), F32),
        scratch_shapes=[pltpu.VMEM(tab.shape, I32), pltpu.SemaphoreType.DMA(())],
        compiler_params=_cparams(("arbitrary",)),
        name="peer_v",
    )(rows, c8, tab)


def _peer(idx_t, gate_t, h2, tab_u, tab_v, tb):
    t = h2.shape[0]
    rows = idx_t.T * ROW_SUB
    c8 = _peer_u(rows, h2.reshape(t, SUBLANES, LANES), gate_t.T, tab_u, tb)
    out = _peer_v(rows, c8, tab_v, tb)
    return out.reshape(t, D_MODEL)


def _final_kernel(x_ref, p_ref, g_ref, y_ref):
    y_ref[...] = _rms(x_ref[...] + p_ref[...]) * g_ref[...]


def _final(x2, peer, g, tb):
    t = x2.shape[0]
    row = pl.BlockSpec((tb, D_MODEL), lambda i: (i, 0))
    return pl.pallas_call(
        _final_kernel,
        grid=(t // tb,),
        in_specs=[row, row, _const_spec((1, D_MODEL))],
        out_specs=row,
        out_shape=jax.ShapeDtypeStruct((t, D_MODEL), F32),
        compiler_params=_cparams(("parallel",)),
        name="final",
    )(x2, peer, g)


def _rope_tables(seq):
    def cos_sin(pos, dim):
        inv = ROPE_THETA ** (-jnp.arange(0, dim, 2, dtype=F32) / dim)
        ang = pos.astype(F32)[:, None] * inv[None, :]
        ang = jnp.concatenate([ang, ang], axis=-1)
        return jnp.cos(ang), jnp.sin(ang)

    sign = jnp.where(jnp.arange(64) < 32, -1.0, 1.0).astype(F32)
    ca, sa = cos_sin(jnp.arange(seq), A_HD)
    rows = seq // GRID_W
    cr, sr = cos_sin(jnp.repeat(jnp.arange(rows), GRID_W), B_HD // 2)
    cc, sc = cos_sin(jnp.tile(jnp.arange(GRID_W), rows), B_HD // 2)
    cos_a = jnp.concatenate([ca, ca], axis=-1)
    sin_a = jnp.concatenate([sa * sign, sa * sign], axis=-1)
    cos_b = jnp.concatenate([cr, cc], axis=-1)
    sin_b = jnp.concatenate([sr * sign, sc * sign], axis=-1)
    return jnp.stack([cos_a, sin_a, cos_b, sin_b])


def _trunk(x, w, tb_proj=256, tb_peer=128):
    b, s, d = x.shape
    t = b * s
    x2d = x.reshape(t, d)
    cs = _rope_tables(s)
    qa, ka, va, qb, kb, vb1 = _inproj(x2d, w["g_mix"], w["w_in"], cs, w["qn"], w["kn"], s, min(tb_proj, s))
    sh = lambda a: a.reshape(b, s, a.shape[-1])
    oa = _window(w["sink"], sh(qa), sh(ka), sh(va))
    ob = _dense(sh(qb), sh(kb), sh(vb1))
    x2, h2, qry = _mixout(x2d, oa.reshape(t, A_Q), ob.reshape(t, B_Q), w["g_mix"], w["w_gate"], w["b_gate"],
                          w["w_oa"], w["w_ob"], w["w_out"], w["g_ffn"], w["w_query"], tb_proj)
    idx_t, gate_t = _topk(qry, w["keys"], tb_proj)
    peer = _peer(idx_t, gate_t, h2, w["tab_u"], w["tab_v"], tb_peer)
    y = _final(x2, peer, w["g_final"], tb_proj)
    return y.reshape(b, s, d)


def kernel(x_prompt, x_sample, g_mix, w_in, w_gate, b_gate, sink, q_norm_g, k_norm_g, w_oa, w_ob, w_out,
           g_ffn, w_query, sub_keys, expert_u, expert_v, g_final):
    assert w_in.shape[0] == 1, "single-layer trunk"
    l = 0
    w = {
        "g_mix": g_mix[l][None, :], "w_in": w_in[l].astype(BF16),
        "w_gate": w_gate[l].astype(BF16), "b_gate": b_gate[l][None, :],
        "sink": sink[l], "qn": q_norm_g[l][None, :], "kn": k_norm_g[l][None, :],
        "w_oa": w_oa[l].astype(BF16), "w_ob": w_ob[l].astype(BF16), "w_out": w_out[l].astype(BF16),
        "g_ffn": g_ffn[l][None, :], "w_query": w_query[l].astype(BF16),
        "keys": sub_keys[l].astype(BF16),
        "tab_u": _pack_table(expert_u[l]), "tab_v": _pack_table(expert_v[l]),
        "g_final": g_final[None, :],
    }
    return _trunk(x_prompt, w), _trunk(x_sample, w)
```

```python
import functools

import jax
import jax.numpy as jnp
from jax import lax
from jax.experimental import pallas as pl
from jax.experimental.pallas import tpu as pltpu

F32 = jnp.float32
BF16 = jnp.bfloat16
I32 = jnp.int32

D_MODEL = 1024
EPS = 1e-6
NEG_INF = -1e30
ROPE_THETA = 10000.0
GRID_W = 64
WINDOW = 128
A_HEADS, A_KV, A_HD = 16, 4, 64
B_HEADS, B_KV, B_HD = 8, 2, 128
A_Q, A_KVW = A_HEADS * A_HD, A_KV * A_HD
B_Q, B_KVW = B_HEADS * B_HD, B_KV * B_HD
IN_COLS = A_Q + 2 * A_KVW + B_Q + 2 * B_KVW
PK_HEADS, N_KEYS, PK_TOPK = 8, 128, 16
PK_DIM = 256
N_PICKS = PK_HEADS * PK_TOPK
LANES = 128
SUBLANES = 8
ROW_WORDS = D_MODEL // 2
ROW_SUB = ROW_WORDS // LANES
PICK_LANES = SUBLANES * N_PICKS
STAGE_TOKENS = 16
LOG2E = 1.4426950408889634
VMEM_LIMIT = 56 * 1024 * 1024


def _cparams(sem):
    return pltpu.CompilerParams(dimension_semantics=sem, vmem_limit_bytes=VMEM_LIMIT)


def _const_spec(shape):
    nd = len(shape)
    return pl.BlockSpec(shape, lambda *_: (0,) * nd, pipeline_mode=pl.Buffered(1))


def _rms(x):
    return x * lax.rsqrt(jnp.mean(x * x, axis=-1, keepdims=True) + EPS)


def _inproj_kernel(x_ref, g_ref, w_ref, cs_ref, qn_ref, kn_ref,
                   qa_ref, ka_ref, va_ref, qb_ref, kb_ref, vb_ref):
    tb = x_ref.shape[0]
    hb = (_rms(x_ref[...]) * g_ref[...]).astype(BF16)
    lane = lax.broadcasted_iota(I32, (tb, LANES), 1)
    first = (lane % 64) < 32
    cos_a, sin_a, cos_b, sin_b = cs_ref[0], cs_ref[1], cs_ref[2], cs_ref[3]

    def rope(xc, cos, sin_signed):
        rot = jnp.where(first, pltpu.roll(xc, 96, 1), pltpu.roll(xc, 32, 1))
        return xc * cos + rot * sin_signed

    def proj(c0, width):
        return jnp.dot(hb, w_ref[:, c0:c0 + width], preferred_element_type=F32)

    for j in range(A_Q // 256):
        y = proj(j * 256, 256)
        for s in range(2):
            yc = rope(y[:, s * LANES:(s + 1) * LANES], cos_a, sin_a) * (A_HD ** -0.5)
            qa_ref[:, j * 256 + s * LANES:j * 256 + (s + 1) * LANES] = yc.astype(BF16)
    y = proj(A_Q, 256)
    for s in range(2):
        ka_ref[:, s * LANES:(s + 1) * LANES] = rope(y[:, s * LANES:(s + 1) * LANES], cos_a, sin_a).astype(BF16)
    va_ref[...] = proj(A_Q + A_KVW, 256).astype(BF16)
    base = A_Q + 2 * A_KVW
    for j in range(B_Q // 256):
        y = proj(base + j * 256, 256)
        for s in range(2):
            yc = _rms(y[:, s * LANES:(s + 1) * LANES]) * qn_ref[...]
            yc = rope(yc, cos_b, sin_b) * (B_HD ** -0.5 * LOG2E)
            qb_ref[:, j * 256 + s * LANES:j * 256 + (s + 1) * LANES] = yc.astype(BF16)
    y = proj(base + B_Q, 256)
    for s in range(2):
        yc = _rms(y[:, s * LANES:(s + 1) * LANES]) * kn_ref[...]
        kb_ref[:, s * LANES:(s + 1) * LANES] = rope(yc, cos_b, sin_b).astype(BF16)
    y = proj(base + B_Q + B_KVW, 256).astype(BF16)
    ones = jnp.ones((tb, B_HD), BF16)
    for kv in range(B_KV):
        vb_ref[:, 2 * kv * B_HD:(2 * kv + 1) * B_HD] = y[:, kv * B_HD:(kv + 1) * B_HD]
        vb_ref[:, (2 * kv + 1) * B_HD:(2 * kv + 2) * B_HD] = ones


def _inproj(x2d, g_mix, w_in, cs, qn, kn, seq, tb):
    t = x2d.shape[0]
    ns = seq // tb
    row = lambda w: pl.BlockSpec((tb, w), lambda i: (i, 0))
    out = lambda w: jax.ShapeDtypeStruct((t, w), BF16)
    return pl.pallas_call(
        _inproj_kernel,
        grid=(t // tb,),
        in_specs=[row(D_MODEL), _const_spec((1, D_MODEL)), _const_spec((D_MODEL, IN_COLS)),
                  pl.BlockSpec((4, tb, LANES), lambda i: (0, i % ns, 0)),
                  _const_spec((1, LANES)), _const_spec((1, LANES))],
        out_specs=[row(A_Q), row(A_KVW), row(A_KVW), row(B_Q), row(B_KVW), row(2 * B_KVW)],
        out_shape=[out(A_Q), out(A_KVW), out(A_KVW), out(B_Q), out(B_KVW), out(2 * B_KVW)],
        compiler_params=_cparams(("parallel",)),
        name="inproj",
    )(x2d, g_mix, w_in, cs, qn, kn)


def _window_kernel(sink_ref, q_ref, kp_ref, kc_ref, kn_ref, vp_ref, vc_ref, vn_ref, o_ref):
    i = pl.program_id(1)
    nb = pl.num_programs(1)
    blk = q_ref.shape[0]
    r = lax.broadcasted_iota(I32, (blk, 3 * blk), 0)
    c = lax.broadcasted_iota(I32, (blk, 3 * blk), 1)
    rel = c - blk - r
    lo_ok = jnp.where(i > 0, 0, blk)
    hi_ok = jnp.where(i < nb - 1, 3 * blk, 2 * blk)
    ok = (jnp.abs(rel) <= WINDOW) & (c >= lo_ok) & (c < hi_ok)
    k3 = jnp.concatenate([kp_ref[...], kc_ref[...], kn_ref[...]], axis=0)
    v3 = jnp.concatenate([vp_ref[...], vc_ref[...], vn_ref[...]], axis=0)
    group = A_HEADS // A_KV
    outs = []
    for h in range(A_HEADS):
        kv = h // group
        qh = q_ref[:, h * A_HD:(h + 1) * A_HD]
        kh = k3[:, kv * A_HD:(kv + 1) * A_HD]
        vh = v3[:, kv * A_HD:(kv + 1) * A_HD]
        s = lax.dot_general(qh, kh, (((1,), (1,)), ((), ())), preferred_element_type=F32)
        s = jnp.where(ok, s, NEG_INF)
        sk = sink_ref[h]
        m = jnp.maximum(jnp.max(s, axis=1, keepdims=True), sk)
        p = jnp.exp(s - m)
        den = jnp.sum(p, axis=1, keepdims=True) + jnp.exp(sk - m)
        o = jnp.dot(p.astype(BF16), vh, preferred_element_type=F32) / den
        outs.append(o.astype(BF16))
    o_ref[...] = jnp.concatenate(outs, axis=1)


def _window(sink, qa, ka, va, blk=128):
    b, s, _ = qa.shape
    nb = s // blk
    qspec = pl.BlockSpec((None, blk, A_Q), lambda bi, i: (bi, i, 0))
    prev = pl.BlockSpec((None, blk, A_KVW), lambda bi, i: (bi, jnp.maximum(i - 1, 0), 0))
    cur = pl.BlockSpec((None, blk, A_KVW), lambda bi, i: (bi, i, 0))
    nxt = pl.BlockSpec((None, blk, A_KVW), lambda bi, i: (bi, jnp.minimum(i + 1, nb - 1), 0))
    return pl.pallas_call(
        _window_kernel,
        grid=(b, nb),
        in_specs=[pl.BlockSpec(memory_space=pltpu.SMEM), qspec, prev, cur, nxt, prev, cur, nxt],
        out_specs=qspec,
        out_shape=jax.ShapeDtypeStruct((b, s, A_Q), BF16),
        compiler_params=_cparams(("parallel", "parallel")),
        name="window",
    )(sink, qa, ka, ka, ka, va, va, va)


def _dense_kernel(q_ref, k_ref, v_ref, o_ref, m_sc, acc_sc, *, tk, rb):
    tq = q_ref.shape[0]
    group = B_HEADS // B_KV
    qs = [q_ref[i * rb:(i + 1) * rb, g * B_HD:(g + 1) * B_HD] for g in range(group) for i in range(tq // rb)]
    m_sc[...] = jnp.full(m_sc.shape, -jnp.inf, F32)
    acc_sc[...] = jnp.zeros(acc_sc.shape, F32)

    def body(j, carry):
        off = pl.multiple_of(j * tk, tk)
        k = k_ref[pl.ds(off, tk), :]
        v = v_ref[pl.ds(off, tk), :]
        for n, q in enumerate(qs):
            rows = slice(n * rb, (n + 1) * rb)
            s = lax.dot_general(q, k, (((1,), (1,)), ((), ())), preferred_element_type=F32)
            m_prev = m_sc[rows]
            m_next = jnp.maximum(m_prev, jnp.max(s, axis=1, keepdims=True))
            p = jnp.exp2(s - jnp.concatenate([m_next] * (tk // LANES), axis=1))
            alpha = jnp.exp2(m_prev - m_next)
            acc_sc[rows] = jnp.concatenate([alpha, alpha], axis=1) * acc_sc[rows] + jnp.dot(
                p.astype(BF16), v, preferred_element_type=F32)
            m_sc[rows] = m_next
        return carry

    lax.fori_loop(0, k_ref.shape[0] // tk, body, 0)
    for g in range(group):
        acc = acc_sc[g * tq:(g + 1) * tq]
        o_ref[:, g * B_HD:(g + 1) * B_HD] = (acc[:, :B_HD] / acc[:, B_HD:]).astype(BF16)


def _dense(qb, kb, vb1, tq=512, tk=1024, rb=256):
    b, s, _ = qb.shape
    tq, tk = min(tq, s), min(tk, s)
    rb = min(rb, tq)
    group = B_HEADS // B_KV
    gw = group * B_HD
    qspec = pl.BlockSpec((None, tq, gw), lambda bi, kv, i: (bi, i, kv))
    kspec = pl.BlockSpec((None, s, B_HD), lambda bi, kv, i: (bi, 0, kv))
    vspec = pl.BlockSpec((None, s, 2 * B_HD), lambda bi, kv, i: (bi, 0, kv))
    return pl.pallas_call(
        functools.partial(_dense_kernel, tk=tk, rb=rb),
        grid=(b, B_KV, s // tq),
        in_specs=[qspec, kspec, vspec],
        out_specs=qspec,
        out_shape=jax.ShapeDtypeStruct((b, s, B_Q), BF16),
        scratch_shapes=[pltpu.VMEM((group * tq, B_HD), F32), pltpu.VMEM((group * tq, 2 * B_HD), F32)],
        compiler_params=_cparams(("parallel", "parallel", "parallel")),
        name="dense",
    )(qb, kb, vb1)


def _mixout_kernel(x_ref, oa_ref, ob_ref, gmix_ref, wg_ref, bg_ref, woa_ref, wob_ref, wout_ref,
                   gffn_ref, wq_ref, x2_ref, h2_ref, qry_ref):
    x = x_ref[...]
    hb = (_rms(x) * gmix_ref[...]).astype(BF16)
    gates = jax.nn.sigmoid(jnp.dot(hb, wg_ref[...], preferred_element_type=F32) + bg_ref[...])
    ya = jnp.dot(oa_ref[...], woa_ref[...], preferred_element_type=F32)
    yb = jnp.dot(ob_ref[...], wob_ref[...], preferred_element_type=F32)
    merged = (gates[:, :D_MODEL] * ya + gates[:, D_MODEL:] * yb).astype(BF16)
    x2 = x + jnp.dot(merged, wout_ref[...], preferred_element_type=F32)
    x2_ref[...] = x2
    h2 = _rms(x2) * gffn_ref[...]
    h2_ref[...] = h2
    qry_ref[...] = jnp.dot(h2.astype(BF16), wq_ref[...], preferred_element_type=F32).astype(BF16)


def _mixout(x2d, oa, ob, gmix, wg, bg, woa, wob, wout, gffn, wq, tb):
    t = x2d.shape[0]
    row = lambda w: pl.BlockSpec((tb, w), lambda i: (i, 0))
    qw = PK_HEADS * PK_DIM
    return pl.pallas_call(
        _mixout_kernel,
        grid=(t // tb,),
        in_specs=[row(D_MODEL), row(A_Q), row(B_Q), _const_spec((1, D_MODEL)),
                  _const_spec((D_MODEL, 2 * D_MODEL)), _const_spec((1, 2 * D_MODEL)),
                  _const_spec((A_Q, D_MODEL)), _const_spec((B_Q, D_MODEL)), _const_spec((D_MODEL, D_MODEL)),
                  _const_spec((1, D_MODEL)), _const_spec((D_MODEL, qw))],
        out_specs=[row(D_MODEL), row(D_MODEL), row(qw)],
        out_shape=[jax.ShapeDtypeStruct((t, D_MODEL), F32), jax.ShapeDtypeStruct((t, D_MODEL), F32),
                   jax.ShapeDtypeStruct((t, qw), BF16)],
        compiler_params=_cparams(("parallel",)),
        name="mixout",
    )(x2d, oa, ob, gmix, wg, bg, woa, wob, wout, gffn, wq)


def _top_rows(vals, ids, k):
    big = float(PK_TOPK * PK_TOPK)
    top_v, top_i = [], []
    for _ in range(k):
        m = jnp.max(vals, axis=0, keepdims=True)
        idx = jnp.min(jnp.where(vals == m, ids, big), axis=0, keepdims=True)
        top_v.append(m)
        top_i.append(idx)
        vals = jnp.where(ids == idx, -jnp.inf, vals)
    return jnp.concatenate(top_v, axis=0), jnp.concatenate(top_i, axis=0)


def _topk_kernel(qry_ref, keys_ref, idx_ref, gate_ref):
    tb = qry_ref.shape[0]
    row_id = lambda n: lax.broadcasted_iota(I32, (n, tb), 0).astype(F32)
    tops = []
    for p in range(2):
        q = qry_ref[:, p * N_KEYS:(p + 1) * N_KEYS]
        sc = lax.dot_general(keys_ref[0, p], q, (((1,), (1,)), ((), ())), preferred_element_type=F32)
        tops.append(_top_rows(sc, row_id(N_KEYS), PK_TOPK))
    (s1, i1), (s2, i2) = tops
    half = PK_TOPK // 2
    cand = [s1[0:1] + s2]
    cpos = [row_id(PK_TOPK)]
    cidx = [i1[0:1] * float(N_KEYS) + i2]
    for a in range(1, half):
        cand.append(s1[a:a + 1] + s2[:half])
        cpos.append(row_id(half) + float(a * PK_TOPK))
        cidx.append(i1[a:a + 1] * float(N_KEYS) + i2[:half])
    cand.append(s1[half:] + s2[0:1])
    cpos.append((row_id(half) + float(half)) * float(PK_TOPK))
    cidx.append(i1[half:] * float(N_KEYS) + i2[0:1])
    cand, cpos, cidx = (jnp.concatenate(x, axis=0) for x in (cand, cpos, cidx))
    best, pos = _top_rows(cand, cpos, PK_TOPK)
    picked = [jnp.max(jnp.where(cpos == pos[j:j + 1], cidx, -1.0), axis=0, keepdims=True)
              for j in range(PK_TOPK)]
    idx_ref[...] = jnp.concatenate(picked, axis=0).astype(I32)
    e = jnp.exp(best - best[0:1])
    gate_ref[...] = e / jnp.sum(e, axis=0, keepdims=True)


def _topk(qry, keys, tb):
    t = qry.shape[0]
    return pl.pallas_call(
        _topk_kernel,
        grid=(t // tb, PK_HEADS),
        in_specs=[pl.BlockSpec((tb, PK_DIM), lambda i, h: (i, h)),
                  pl.BlockSpec((1, 2, N_KEYS, PK_DIM // 2), lambda i, h: (h, 0, 0, 0))],
        out_specs=[pl.BlockSpec((PK_TOPK, tb), lambda i, h: (h, i)),
                   pl.BlockSpec((PK_TOPK, tb), lambda i, h: (h, i))],
        out_shape=[jax.ShapeDtypeStruct((N_PICKS, t), I32), jax.ShapeDtypeStruct((N_PICKS, t), F32)],
        compiler_params=_cparams(("parallel", "parallel")),
        name="topk",
    )(qry, keys)


def _pack_table(w):
    bits = lax.bitcast_convert_type(w.astype(BF16), jnp.uint16).astype(jnp.uint32)
    bits = bits.reshape(w.shape[0], ROW_SUB, 2, LANES)
    packed = bits[:, :, 0] | (bits[:, :, 1] << 16)
    return lax.bitcast_convert_type(packed, I32).reshape(w.shape[0] * ROW_SUB, LANES)


def _split3(x):
    hi = x.astype(BF16).astype(F32)
    r1 = x - hi
    mid = r1.astype(BF16).astype(F32)
    lo = (r1 - mid).astype(BF16).astype(F32)
    return hi, mid, lo


def _load_table(tab_hbm, tab_vmem, sem):
    @pl.when(pl.program_id(0) == 0)
    def _():
        cp = pltpu.make_async_copy(tab_hbm, tab_vmem, sem)
        cp.start()
        cp.wait()


def _weights(stage, b, u, tab):
    rows = [tab[pl.ds(pl.multiple_of(stage[b, u * N_PICKS + k], ROW_SUB), ROW_SUB), :] for k in range(N_PICKS)]
    return pltpu.bitcast(jnp.concatenate(rows, axis=0), BF16)


def _staged_token_loop(rows_hbm, stage, sem, tb, token):
    step = pl.program_id(0)
    per_step = tb // STAGE_TOKENS
    total = pl.num_programs(0) * per_step

    def copy(g, b):
        return pltpu.make_async_copy(rows_hbm.at[g], stage.at[b], sem.at[b])

    @pl.when(step == 0)
    def _():
        copy(0, 0).start()

    def body(i, carry):
        for b in range(2):
            j = 2 * i + b
            g = step * per_step + j
            copy(g, b).wait()

            @pl.when(g + 1 < total)
            def _():
                copy(g + 1, 1 - b).start()

            for u in range(STAGE_TOKENS):
                token(j * STAGE_TOKENS + u, b, u)
        return carry

    lax.fori_loop(0, per_step // 2, body, 0)


def _stage_scratch():
    return [pltpu.SMEM((2, STAGE_TOKENS * N_PICKS), I32), pltpu.SemaphoreType.DMA((2,))]


def _lhs3(x):
    hi, mid, lo = _split3(x)
    return jnp.concatenate([hi, mid, lo, jnp.zeros_like(x)], axis=0).astype(BF16)


def _diag_mask():
    shape = (SUBLANES, PICK_LANES)
    return (lax.broadcasted_iota(I32, shape, 1) % SUBLANES) == lax.broadcasted_iota(I32, shape, 0)


def _group_allsum(x):
    n = x.shape[1]
    lane = lax.broadcasted_iota(I32, x.shape, 1)
    dist = 1
    while dist < SUBLANES:
        x = x + jnp.where((lane & dist) == 0, pltpu.roll(x, n - dist, 1), pltpu.roll(x, dist, 1))
        dist *= 2
    return x


def _peer_u_kernel(rows_hbm, h_ref, gate_ref, tab_hbm, c_ref, tab, dbuf, sem, stage, stage_sem):
    _load_table(tab_hbm, tab, sem)
    diag = _diag_mask()

    def token(t, b, u):
        w = _weights(stage, b, u, tab)
        r = lax.dot_general(_lhs3(h_ref[t]), w, (((1,), (1,)), ((), ())), preferred_element_type=F32)
        rc = r[0:8] + r[8:16] + r[16:24]
        dbuf[pl.ds(t, 1), :] = jnp.sum(jnp.where(diag, rc, 0.0), axis=0, keepdims=True)

    _staged_token_loop(rows_hbm, stage, stage_sem, h_ref.shape[0], token)
    d = _group_allsum(dbuf[...])
    shape = (N_PICKS, PICK_LANES)
    spread = (lax.broadcasted_iota(I32, shape, 1) // SUBLANES == lax.broadcasted_iota(I32, shape, 0)).astype(BF16)
    g8 = sum(jnp.dot(p.astype(BF16), spread, preferred_element_type=F32) for p in _split3(gate_ref[...]))
    c_ref[...] = g8 * (0.5 * d * (1.0 + lax.erf(d * (0.5 ** 0.5))))


def _peer_u(rows, h3, gate, tab, tb):
    t = h3.shape[0]
    return pl.pallas_call(
        _peer_u_kernel,
        grid=(t // tb,),
        in_specs=[pl.BlockSpec(memory_space=pl.ANY),
                  pl.BlockSpec((tb, SUBLANES, LANES), lambda i: (i, 0, 0)),
                  pl.BlockSpec((tb, N_PICKS), lambda i: (i, 0)),
                  pl.BlockSpec(memory_space=pl.ANY)],
        out_specs=pl.BlockSpec((tb, PICK_LANES), lambda i: (i, 0)),
        out_shape=jax.ShapeDtypeStruct((t, PICK_LANES), F32),
        scratch_shapes=[pltpu.VMEM(tab.shape, I32), pltpu.VMEM((tb, PICK_LANES), F32),
                        pltpu.SemaphoreType.DMA(())] + _stage_scratch(),
        compiler_params=_cparams(("arbitrary",)),
        name="peer_u",
    )(rows, h3, gate, tab)


def _peer_v_kernel(rows_hbm, c_ref, tab_hbm, out_ref, tab, sem, stage, stage_sem):
    _load_table(tab_hbm, tab, sem)
    diag = _diag_mask()

    def token(t, b, u):
        w = _weights(stage, b, u, tab)
        c8 = jnp.where(diag, c_ref[pl.ds(t, 1), :], 0.0)
        r = jnp.dot(_lhs3(c8), w, preferred_element_type=F32)
        out_ref[t] = r[0:8] + r[8:16] + r[16:24]

    _staged_token_loop(rows_hbm, stage, stage_sem, out_ref.shape[0], token)


def _peer_v(rows, c8, tab, tb):
    t = c8.shape[0]
    return pl.pallas_call(
        _peer_v_kernel,
        grid=(t // tb,),
        in_specs=[pl.BlockSpec(memory_space=pl.ANY),
                  pl.BlockSpec((tb, PICK_LANES), lambda i: (i, 0)),
                  pl.BlockSpec(memory_space=pl.ANY)],
        out_specs=pl.BlockSpec((tb, SUBLANES, LANES), lambda i: (i, 0, 0)),
        out_shape=jax.ShapeDtypeStruct((t, SUBLANES, LANES), F32),
        scratch_shapes=[pltpu.VMEM(tab.shape, I32), pltpu.SemaphoreType.DMA(())] + _stage_scratch(),
        compiler_params=_cparams(("arbitrary",)),
        name="peer_v",
    )(rows, c8, tab)


def _peer(idx_t, gate_t, h2, tab_u, tab_v, tb):
    t = h2.shape[0]
    assert tb % (2 * STAGE_TOKENS) == 0 and t % tb == 0
    rows = (idx_t.T * ROW_SUB).reshape(t // STAGE_TOKENS, STAGE_TOKENS * N_PICKS)
    c8 = _peer_u(rows, h2.reshape(t, SUBLANES, LANES), gate_t.T, tab_u, tb)
    out = _peer_v(rows, c8, tab_v, tb)
    return out.reshape(t, D_MODEL)


def _final_kernel(x_ref, p_ref, g_ref, y_ref):
    y_ref[...] = _rms(x_ref[...] + p_ref[...]) * g_ref[...]


def _final(x2, peer, g, tb):
    t = x2.shape[0]
    row = pl.BlockSpec((tb, D_MODEL), lambda i: (i, 0))
    return pl.pallas_call(
        _final_kernel,
        grid=(t // tb,),
        in_specs=[row, row, _const_spec((1, D_MODEL))],
        out_specs=row,
        out_shape=jax.ShapeDtypeStruct((t, D_MODEL), F32),
        compiler_params=_cparams(("parallel",)),
        name="final",
    )(x2, peer, g)


def _rope_tables(seq):
    def cos_sin(pos, dim):
        inv = ROPE_THETA ** (-jnp.arange(0, dim, 2, dtype=F32) / dim)
        ang = pos.astype(F32)[:, None] * inv[None, :]
        ang = jnp.concatenate([ang, ang], axis=-1)
        return jnp.cos(ang), jnp.sin(ang)

    sign = jnp.where(jnp.arange(64) < 32, -1.0, 1.0).astype(F32)
    ca, sa = cos_sin(jnp.arange(seq), A_HD)
    rows = seq // GRID_W
    cr, sr = cos_sin(jnp.repeat(jnp.arange(rows), GRID_W), B_HD // 2)
    cc, sc = cos_sin(jnp.tile(jnp.arange(GRID_W), rows), B_HD // 2)
    cos_a = jnp.concatenate([ca, ca], axis=-1)
    sin_a = jnp.concatenate([sa * sign, sa * sign], axis=-1)
    cos_b = jnp.concatenate([cr, cc], axis=-1)
    sin_b = jnp.concatenate([sr * sign, sc * sign], axis=-1)
    return jnp.stack([cos_a, sin_a, cos_b, sin_b])


def _trunk(x, w, tb_proj=256, tb_peer=128):
    b, s, d = x.shape
    t = b * s
    x2d = x.reshape(t, d)
    cs = _rope_tables(s)
    qa, ka, va, qb, kb, vb1 = _inproj(x2d, w["g_mix"], w["w_in"], cs, w["qn"], w["kn"], s, min(tb_proj, s))
    sh = lambda a: a.reshape(b, s, a.shape[-1])
    oa = _window(w["sink"], sh(qa), sh(ka), sh(va))
    ob = _dense(sh(qb), sh(kb), sh(vb1))
    x2, h2, qry = _mixout(x2d, oa.reshape(t, A_Q), ob.reshape(t, B_Q), w["g_mix"], w["w_gate"], w["b_gate"],
                          w["w_oa"], w["w_ob"], w["w_out"], w["g_ffn"], w["w_query"], tb_proj)
    idx_t, gate_t = _topk(qry, w["keys"], tb_proj)
    peer = _peer(idx_t, gate_t, h2, w["tab_u"], w["tab_v"], tb_peer)
    y = _final(x2, peer, w["g_final"], tb_proj)
    return y.reshape(b, s, d)


def kernel(x_prompt, x_sample, g_mix, w_in, w_gate, b_gate, sink, q_norm_g, k_norm_g, w_oa, w_ob, w_out,
           g_ffn, w_query, sub_keys, expert_u, expert_v, g_final):
    assert w_in.shape[0] == 1, "single-layer trunk"
    l = 0
    w = {
        "g_mix": g_mix[l][None, :], "w_in": w_in[l].astype(BF16),
        "w_gate": w_gate[l].astype(BF16), "b_gate": b_gate[l][None, :],
        "sink": sink[l], "qn": q_norm_g[l][None, :], "kn": k_norm_g[l][None, :],
        "w_oa": w_oa[l].astype(BF16), "w_ob": w_ob[l].astype(BF16), "w_out": w_out[l].astype(BF16),
        "g_ffn": g_ffn[l][None, :], "w_query": w_query[l].astype(BF16),
        "keys": sub_keys[l].astype(BF16),
        "tab_u": _pack_table(expert_u[l]), "tab_v": _pack_table(expert_v[l]),
        "g_final": g_final[None, :],
    }
    return _trunk(x_prompt, w), _trunk(x_sample, w)
```

```python
import functools

import jax
import jax.numpy as jnp
from jax import lax
from jax.experimental import pallas as pl
from jax.experimental.pallas import tpu as pltpu

F32 = jnp.float32
BF16 = jnp.bfloat16
I32 = jnp.int32

D_MODEL = 1024
EPS = 1e-6
NEG_INF = -1e30
ROPE_THETA = 10000.0
GRID_W = 64
WINDOW = 128
A_HEADS, A_KV, A_HD = 16, 4, 64
B_HEADS, B_KV, B_HD = 8, 2, 128
A_Q, A_KVW = A_HEADS * A_HD, A_KV * A_HD
B_Q, B_KVW = B_HEADS * B_HD, B_KV * B_HD
IN_COLS = A_Q + 2 * A_KVW + B_Q + 2 * B_KVW
PK_HEADS, N_KEYS, PK_TOPK = 8, 128, 16
PK_DIM = 256
N_PICKS = PK_HEADS * PK_TOPK
LANES = 128
SUBLANES = 8
ROW_WORDS = D_MODEL // 2
ROW_SUB = ROW_WORDS // LANES
PICK_LANES = SUBLANES * N_PICKS
STAGE_TOKENS = 16
STAGE_BUFS = 4
LOG2E = 1.4426950408889634
VMEM_LIMIT = 56 * 1024 * 1024


def _cparams(sem):
    return pltpu.CompilerParams(dimension_semantics=sem, vmem_limit_bytes=VMEM_LIMIT)


def _const_spec(shape):
    nd = len(shape)
    return pl.BlockSpec(shape, lambda *_: (0,) * nd, pipeline_mode=pl.Buffered(1))


def _rms(x):
    return x * lax.rsqrt(jnp.mean(x * x, axis=-1, keepdims=True) + EPS)


def _inproj_kernel(x_ref, g_ref, w_ref, cs_ref, qn_ref, kn_ref,
                   qa_ref, ka_ref, va_ref, qb_ref, kb_ref, vb_ref):
    tb = x_ref.shape[0]
    hb = (_rms(x_ref[...]) * g_ref[...]).astype(BF16)
    lane = lax.broadcasted_iota(I32, (tb, LANES), 1)
    first = (lane % 64) < 32
    cos_a, sin_a, cos_b, sin_b = cs_ref[0], cs_ref[1], cs_ref[2], cs_ref[3]

    def rope(xc, cos, sin_signed):
        rot = jnp.where(first, pltpu.roll(xc, 96, 1), pltpu.roll(xc, 32, 1))
        return xc * cos + rot * sin_signed

    def proj(c0, width):
        return jnp.dot(hb, w_ref[:, c0:c0 + width], preferred_element_type=F32)

    for j in range(A_Q // 256):
        y = proj(j * 256, 256)
        for s in range(2):
            yc = rope(y[:, s * LANES:(s + 1) * LANES], cos_a, sin_a) * (A_HD ** -0.5)
            qa_ref[:, j * 256 + s * LANES:j * 256 + (s + 1) * LANES] = yc.astype(BF16)
    y = proj(A_Q, 256)
    for s in range(2):
        ka_ref[:, s * LANES:(s + 1) * LANES] = rope(y[:, s * LANES:(s + 1) * LANES], cos_a, sin_a).astype(BF16)
    va_ref[...] = proj(A_Q + A_KVW, 256).astype(BF16)
    base = A_Q + 2 * A_KVW
    for j in range(B_Q // 256):
        y = proj(base + j * 256, 256)
        for s in range(2):
            yc = _rms(y[:, s * LANES:(s + 1) * LANES]) * qn_ref[...]
            yc = rope(yc, cos_b, sin_b) * (B_HD ** -0.5 * LOG2E)
            qb_ref[:, j * 256 + s * LANES:j * 256 + (s + 1) * LANES] = yc.astype(BF16)
    y = proj(base + B_Q, 256)
    for s in range(2):
        yc = _rms(y[:, s * LANES:(s + 1) * LANES]) * kn_ref[...]
        kb_ref[:, s * LANES:(s + 1) * LANES] = rope(yc, cos_b, sin_b).astype(BF16)
    y = proj(base + B_Q + B_KVW, 256).astype(BF16)
    ones = jnp.ones((tb, B_HD), BF16)
    for kv in range(B_KV):
        vb_ref[:, 2 * kv * B_HD:(2 * kv + 1) * B_HD] = y[:, kv * B_HD:(kv + 1) * B_HD]
        vb_ref[:, (2 * kv + 1) * B_HD:(2 * kv + 2) * B_HD] = ones


def _inproj(x2d, g_mix, w_in, cs, qn, kn, seq, tb):
    t = x2d.shape[0]
    ns = seq // tb
    row = lambda w: pl.BlockSpec((tb, w), lambda i: (i, 0))
    out = lambda w: jax.ShapeDtypeStruct((t, w), BF16)
    return pl.pallas_call(
        _inproj_kernel,
        grid=(t // tb,),
        in_specs=[row(D_MODEL), _const_spec((1, D_MODEL)), _const_spec((D_MODEL, IN_COLS)),
                  pl.BlockSpec((4, tb, LANES), lambda i: (0, i % ns, 0)),
                  _const_spec((1, LANES)), _const_spec((1, LANES))],
        out_specs=[row(A_Q), row(A_KVW), row(A_KVW), row(B_Q), row(B_KVW), row(2 * B_KVW)],
        out_shape=[out(A_Q), out(A_KVW), out(A_KVW), out(B_Q), out(B_KVW), out(2 * B_KVW)],
        compiler_params=_cparams(("parallel",)),
        name="inproj",
    )(x2d, g_mix, w_in, cs, qn, kn)


def _window_kernel(sink_ref, q_ref, kp_ref, kc_ref, kn_ref, vp_ref, vc_ref, vn_ref, o_ref):
    i = pl.program_id(1)
    nb = pl.num_programs(1)
    blk = q_ref.shape[0]
    r = lax.broadcasted_iota(I32, (blk, 3 * blk), 0)
    c = lax.broadcasted_iota(I32, (blk, 3 * blk), 1)
    rel = c - blk - r
    lo_ok = jnp.where(i > 0, 0, blk)
    hi_ok = jnp.where(i < nb - 1, 3 * blk, 2 * blk)
    ok = (jnp.abs(rel) <= WINDOW) & (c >= lo_ok) & (c < hi_ok)
    k3 = jnp.concatenate([kp_ref[...], kc_ref[...], kn_ref[...]], axis=0)
    v3 = jnp.concatenate([vp_ref[...], vc_ref[...], vn_ref[...]], axis=0)
    group = A_HEADS // A_KV
    outs = []
    for h in range(A_HEADS):
        kv = h // group
        qh = q_ref[:, h * A_HD:(h + 1) * A_HD]
        kh = k3[:, kv * A_HD:(kv + 1) * A_HD]
        vh = v3[:, kv * A_HD:(kv + 1) * A_HD]
        s = lax.dot_general(qh, kh, (((1,), (1,)), ((), ())), preferred_element_type=F32)
        s = jnp.where(ok, s, NEG_INF)
        sk = sink_ref[h]
        m = jnp.maximum(jnp.max(s, axis=1, keepdims=True), sk)
        p = jnp.exp(s - m)
        den = jnp.sum(p, axis=1, keepdims=True) + jnp.exp(sk - m)
        o = jnp.dot(p.astype(BF16), vh, preferred_element_type=F32) / den
        outs.append(o.astype(BF16))
    o_ref[...] = jnp.concatenate(outs, axis=1)


def _window(sink, qa, ka, va, blk=128):
    b, s, _ = qa.shape
    nb = s // blk
    qspec = pl.BlockSpec((None, blk, A_Q), lambda bi, i: (bi, i, 0))
    prev = pl.BlockSpec((None, blk, A_KVW), lambda bi, i: (bi, jnp.maximum(i - 1, 0), 0))
    cur = pl.BlockSpec((None, blk, A_KVW), lambda bi, i: (bi, i, 0))
    nxt = pl.BlockSpec((None, blk, A_KVW), lambda bi, i: (bi, jnp.minimum(i + 1, nb - 1), 0))
    return pl.pallas_call(
        _window_kernel,
        grid=(b, nb),
        in_specs=[pl.BlockSpec(memory_space=pltpu.SMEM), qspec, prev, cur, nxt, prev, cur, nxt],
        out_specs=qspec,
        out_shape=jax.ShapeDtypeStruct((b, s, A_Q), BF16),
        compiler_params=_cparams(("parallel", "parallel")),
        name="window",
    )(sink, qa, ka, ka, ka, va, va, va)


def _dense_kernel(q_ref, k_ref, v_ref, o_ref, m_sc, acc_sc, *, tk, rb):
    tq = q_ref.shape[0]
    group = B_HEADS // B_KV
    qs = [q_ref[i * rb:(i + 1) * rb, g * B_HD:(g + 1) * B_HD] for g in range(group) for i in range(tq // rb)]
    m_sc[...] = jnp.full(m_sc.shape, -jnp.inf, F32)
    acc_sc[...] = jnp.zeros(acc_sc.shape, F32)

    def body(j, carry):
        off = pl.multiple_of(j * tk, tk)
        k = k_ref[pl.ds(off, tk), :]
        v = v_ref[pl.ds(off, tk), :]
        for n, q in enumerate(qs):
            rows = slice(n * rb, (n + 1) * rb)
            s = lax.dot_general(q, k, (((1,), (1,)), ((), ())), preferred_element_type=F32)
            m_prev = m_sc[rows]
            m_next = jnp.maximum(m_prev, jnp.max(s, axis=1, keepdims=True))
            p = jnp.exp2(s - jnp.concatenate([m_next] * (tk // LANES), axis=1))
            alpha = jnp.exp2(m_prev - m_next)
            acc_sc[rows] = jnp.concatenate([alpha, alpha], axis=1) * acc_sc[rows] + jnp.dot(
                p.astype(BF16), v, preferred_element_type=F32)
            m_sc[rows] = m_next
        return carry

    lax.fori_loop(0, k_ref.shape[0] // tk, body, 0)
    for g in range(group):
        acc = acc_sc[g * tq:(g + 1) * tq]
        o_ref[:, g * B_HD:(g + 1) * B_HD] = (acc[:, :B_HD] / acc[:, B_HD:]).astype(BF16)


def _dense(qb, kb, vb1, tq=512, tk=1024, rb=256):
    b, s, _ = qb.shape
    tq, tk = min(tq, s), min(tk, s)
    rb = min(rb, tq)
    group = B_HEADS // B_KV
    gw = group * B_HD
    qspec = pl.BlockSpec((None, tq, gw), lambda bi, kv, i: (bi, i, kv))
    kspec = pl.BlockSpec((None, s, B_HD), lambda bi, kv, i: (bi, 0, kv))
    vspec = pl.BlockSpec((None, s, 2 * B_HD), lambda bi, kv, i: (bi, 0, kv))
    return pl.pallas_call(
        functools.partial(_dense_kernel, tk=tk, rb=rb),
        grid=(b, B_KV, s // tq),
        in_specs=[qspec, kspec, vspec],
        out_specs=qspec,
        out_shape=jax.ShapeDtypeStruct((b, s, B_Q), BF16),
        scratch_shapes=[pltpu.VMEM((group * tq, B_HD), F32), pltpu.VMEM((group * tq, 2 * B_HD), F32)],
        compiler_params=_cparams(("parallel", "parallel", "parallel")),
        name="dense",
    )(qb, kb, vb1)


def _mixout_kernel(x_ref, oa_ref, ob_ref, gmix_ref, wg_ref, bg_ref, woa_ref, wob_ref, wout_ref,
                   gffn_ref, wq_ref, x2_ref, h2_ref, qry_ref):
    x = x_ref[...]
    hb = (_rms(x) * gmix_ref[...]).astype(BF16)
    gates = jax.nn.sigmoid(jnp.dot(hb, wg_ref[...], preferred_element_type=F32) + bg_ref[...])
    ya = jnp.dot(oa_ref[...], woa_ref[...], preferred_element_type=F32)
    yb = jnp.dot(ob_ref[...], wob_ref[...], preferred_element_type=F32)
    merged = (gates[:, :D_MODEL] * ya + gates[:, D_MODEL:] * yb).astype(BF16)
    x2 = x + jnp.dot(merged, wout_ref[...], preferred_element_type=F32)
    x2_ref[...] = x2
    h2 = _rms(x2) * gffn_ref[...]
    h2_ref[...] = h2
    qry_ref[...] = jnp.dot(h2.astype(BF16), wq_ref[...], preferred_element_type=F32).astype(BF16)


def _mixout(x2d, oa, ob, gmix, wg, bg, woa, wob, wout, gffn, wq, tb):
    t = x2d.shape[0]
    row = lambda w: pl.BlockSpec((tb, w), lambda i: (i, 0))
    qw = PK_HEADS * PK_DIM
    return pl.pallas_call(
        _mixout_kernel,
        grid=(t // tb,),
        in_specs=[row(D_MODEL), row(A_Q), row(B_Q), _const_spec((1, D_MODEL)),
                  _const_spec((D_MODEL, 2 * D_MODEL)), _const_spec((1, 2 * D_MODEL)),
                  _const_spec((A_Q, D_MODEL)), _const_spec((B_Q, D_MODEL)), _const_spec((D_MODEL, D_MODEL)),
                  _const_spec((1, D_MODEL)), _const_spec((D_MODEL, qw))],
        out_specs=[row(D_MODEL), row(D_MODEL), row(qw)],
        out_shape=[jax.ShapeDtypeStruct((t, D_MODEL), F32), jax.ShapeDtypeStruct((t, D_MODEL), F32),
                   jax.ShapeDtypeStruct((t, qw), BF16)],
        compiler_params=_cparams(("parallel",)),
        name="mixout",
    )(x2d, oa, ob, gmix, wg, bg, woa, wob, wout, gffn, wq)


def _top_rows(vals, ids, k):
    big = float(PK_TOPK * PK_TOPK)
    top_v, top_i = [], []
    for _ in range(k):
        m = jnp.max(vals, axis=0, keepdims=True)
        idx = jnp.min(jnp.where(vals == m, ids, big), axis=0, keepdims=True)
        top_v.append(m)
        top_i.append(idx)
        vals = jnp.where(ids == idx, -jnp.inf, vals)
    return jnp.concatenate(top_v, axis=0), jnp.concatenate(top_i, axis=0)


def _topk_kernel(qry_ref, keys_ref, idx_ref, gate_ref):
    tb = qry_ref.shape[0]
    row_id = lambda n: lax.broadcasted_iota(I32, (n, tb), 0).astype(F32)
    tops = []
    for p in range(2):
        q = qry_ref[:, p * N_KEYS:(p + 1) * N_KEYS]
        sc = lax.dot_general(keys_ref[0, p], q, (((1,), (1,)), ((), ())), preferred_element_type=F32)
        tops.append(_top_rows(sc, row_id(N_KEYS), PK_TOPK))
    (s1, i1), (s2, i2) = tops
    half = PK_TOPK // 2
    cand = [s1[0:1] + s2]
    cpos = [row_id(PK_TOPK)]
    cidx = [i1[0:1] * float(N_KEYS) + i2]
    for a in range(1, half):
        cand.append(s1[a:a + 1] + s2[:half])
        cpos.append(row_id(half) + float(a * PK_TOPK))
        cidx.append(i1[a:a + 1] * float(N_KEYS) + i2[:half])
    cand.append(s1[half:] + s2[0:1])
    cpos.append((row_id(half) + float(half)) * float(PK_TOPK))
    cidx.append(i1[half:] * float(N_KEYS) + i2[0:1])
    cand, cpos, cidx = (jnp.concatenate(x, axis=0) for x in (cand, cpos, cidx))
    best, pos = _top_rows(cand, cpos, PK_TOPK)
    picked = [jnp.max(jnp.where(cpos == pos[j:j + 1], cidx, -1.0), axis=0, keepdims=True)
              for j in range(PK_TOPK)]
    idx_ref[...] = jnp.concatenate(picked, axis=0).astype(I32)
    e = jnp.exp(best - best[0:1])
    gate_ref[...] = e / jnp.sum(e, axis=0, keepdims=True)


def _topk(qry, keys, tb):
    t = qry.shape[0]
    return pl.pallas_call(
        _topk_kernel,
        grid=(t // tb, PK_HEADS),
        in_specs=[pl.BlockSpec((tb, PK_DIM), lambda i, h: (i, h)),
                  pl.BlockSpec((1, 2, N_KEYS, PK_DIM // 2), lambda i, h: (h, 0, 0, 0))],
        out_specs=[pl.BlockSpec((PK_TOPK, tb), lambda i, h: (h, i)),
                   pl.BlockSpec((PK_TOPK, tb), lambda i, h: (h, i))],
        out_shape=[jax.ShapeDtypeStruct((N_PICKS, t), I32), jax.ShapeDtypeStruct((N_PICKS, t), F32)],
        compiler_params=_cparams(("parallel", "parallel")),
        name="topk",
    )(qry, keys)


def _pack_table(w):
    bits = lax.bitcast_convert_type(w.astype(BF16), jnp.uint16).astype(jnp.uint32)
    bits = bits.reshape(w.shape[0], ROW_SUB, 2, LANES)
    packed = bits[:, :, 0] | (bits[:, :, 1] << 16)
    return lax.bitcast_convert_type(packed, I32).reshape(w.shape[0] * ROW_SUB, LANES)


def _split3(x):
    hi = x.astype(BF16).astype(F32)
    r1 = x - hi
    mid = r1.astype(BF16).astype(F32)
    lo = (r1 - mid).astype(BF16).astype(F32)
    return hi, mid, lo


def _load_table(tab_hbm, tab_vmem, sem):
    @pl.when(pl.program_id(0) == 0)
    def _():
        cp = pltpu.make_async_copy(tab_hbm, tab_vmem, sem)
        cp.start()
        cp.wait()


def _weights(stage, b, u, tab):
    rows = [tab[pl.ds(pl.multiple_of(stage[b, u * N_PICKS + k], ROW_SUB), ROW_SUB), :] for k in range(N_PICKS)]
    return pltpu.bitcast(jnp.concatenate(rows, axis=0), BF16)


def _staged_token_loop(rows_hbm, stage, sem, tb, token):
    step = pl.program_id(0)
    per_step = tb // STAGE_TOKENS
    total = pl.num_programs(0) * per_step
    ahead = STAGE_BUFS - 1

    def copy(g, b):
        return pltpu.make_async_copy(rows_hbm.at[g], stage.at[b], sem.at[b])

    @pl.when(step == 0)
    def _():
        for g in range(ahead):
            copy(g, g).start()

    def body(i, carry):
        for b in range(STAGE_BUFS):
            j = STAGE_BUFS * i + b
            g = step * per_step + j
            copy(g, b).wait()

            @pl.when(g + ahead < total)
            def _():
                copy(g + ahead, (b + ahead) % STAGE_BUFS).start()

            for u in range(STAGE_TOKENS):
                token(j * STAGE_TOKENS + u, b, u)
        return carry

    lax.fori_loop(0, per_step // STAGE_BUFS, body, 0)


def _stage_scratch():
    return [pltpu.SMEM((STAGE_BUFS, STAGE_TOKENS * N_PICKS), I32), pltpu.SemaphoreType.DMA((STAGE_BUFS,))]


def _lhs3(x):
    hi, mid, lo = _split3(x)
    return jnp.concatenate([hi, mid, lo, jnp.zeros_like(x)], axis=0).astype(BF16)


def _diag_mask():
    shape = (SUBLANES, PICK_LANES)
    return (lax.broadcasted_iota(I32, shape, 1) % SUBLANES) == lax.broadcasted_iota(I32, shape, 0)


def _group_allsum(x):
    n = x.shape[1]
    lane = lax.broadcasted_iota(I32, x.shape, 1)
    dist = 1
    while dist < SUBLANES:
        x = x + jnp.where((lane & dist) == 0, pltpu.roll(x, n - dist, 1), pltpu.roll(x, dist, 1))
        dist *= 2
    return x


def _peer_u_kernel(rows_hbm, h_ref, gate_ref, tab_hbm, c_ref, tab, dbuf, sem, stage, stage_sem):
    _load_table(tab_hbm, tab, sem)
    diag = _diag_mask()

    def token(t, b, u):
        w = _weights(stage, b, u, tab)
        r = lax.dot_general(_lhs3(h_ref[t]), w, (((1,), (1,)), ((), ())), preferred_element_type=F32)
        rc = r[0:8] + r[8:16] + r[16:24]
        dbuf[pl.ds(t, 1), :] = jnp.sum(jnp.where(diag, rc, 0.0), axis=0, keepdims=True)

    _staged_token_loop(rows_hbm, stage, stage_sem, h_ref.shape[0], token)
    d = _group_allsum(dbuf[...])
    shape = (N_PICKS, PICK_LANES)
    spread = (lax.broadcasted_iota(I32, shape, 1) // SUBLANES == lax.broadcasted_iota(I32, shape, 0)).astype(BF16)
    g8 = sum(jnp.dot(p.astype(BF16), spread, preferred_element_type=F32) for p in _split3(gate_ref[...]))
    c_ref[...] = g8 * (0.5 * d * (1.0 + lax.erf(d * (0.5 ** 0.5))))


def _peer_u(rows, h3, gate, tab, tb):
    t = h3.shape[0]
    return pl.pallas_call(
        _peer_u_kernel,
        grid=(t // tb,),
        in_specs=[pl.BlockSpec(memory_space=pl.ANY),
                  pl.BlockSpec((tb, SUBLANES, LANES), lambda i: (i, 0, 0)),
                  pl.BlockSpec((tb, N_PICKS), lambda i: (i, 0)),
                  pl.BlockSpec(memory_space=pl.ANY)],
        out_specs=pl.BlockSpec((tb, PICK_LANES), lambda i: (i, 0)),
        out_shape=jax.ShapeDtypeStruct((t, PICK_LANES), F32),
        scratch_shapes=[pltpu.VMEM(tab.shape, I32), pltpu.VMEM((tb, PICK_LANES), F32),
                        pltpu.SemaphoreType.DMA(())] + _stage_scratch(),
        compiler_params=_cparams(("arbitrary",)),
        name="peer_u",
    )(rows, h3, gate, tab)


def _peer_v_kernel(rows_hbm, c_ref, tab_hbm, out_ref, tab, sem, stage, stage_sem):
    _load_table(tab_hbm, tab, sem)
    diag = _diag_mask()

    def token(t, b, u):
        w = _weights(stage, b, u, tab)
        c8 = jnp.where(diag, c_ref[pl.ds(t, 1), :], 0.0)
        r = jnp.dot(_lhs3(c8), w, preferred_element_type=F32)
        out_ref[t] = r[0:8] + r[8:16] + r[16:24]

    _staged_token_loop(rows_hbm, stage, stage_sem, out_ref.shape[0], token)


def _peer_v(rows, c8, tab, tb):
    t = c8.shape[0]
    return pl.pallas_call(
        _peer_v_kernel,
        grid=(t // tb,),
        in_specs=[pl.BlockSpec(memory_space=pl.ANY),
                  pl.BlockSpec((tb, PICK_LANES), lambda i: (i, 0)),
                  pl.BlockSpec(memory_space=pl.ANY)],
        out_specs=pl.BlockSpec((tb, SUBLANES, LANES), lambda i: (i, 0, 0)),
        out_shape=jax.ShapeDtypeStruct((t, SUBLANES, LANES), F32),
        scratch_shapes=[pltpu.VMEM(tab.shape, I32), pltpu.SemaphoreType.DMA(())] + _stage_scratch(),
        compiler_params=_cparams(("arbitrary",)),
        name="peer_v",
    )(rows, c8, tab)


def _peer(idx_t, gate_t, h2, tab_u, tab_v, tb):
    t = h2.shape[0]
    assert tb % (STAGE_BUFS * STAGE_TOKENS) == 0 and t % tb == 0
    rows = (idx_t.T * ROW_SUB).reshape(t // STAGE_TOKENS, STAGE_TOKENS * N_PICKS)
    c8 = _peer_u(rows, h2.reshape(t, SUBLANES, LANES), gate_t.T, tab_u, tb)
    out = _peer_v(rows, c8, tab_v, tb)
    return out.reshape(t, D_MODEL)


def _final_kernel(x_ref, p_ref, g_ref, y_ref):
    y_ref[...] = _rms(x_ref[...] + p_ref[...]) * g_ref[...]


def _final(x2, peer, g, tb):
    t = x2.shape[0]
    row = pl.BlockSpec((tb, D_MODEL), lambda i: (i, 0))
    return pl.pallas_call(
        _final_kernel,
        grid=(t // tb,),
        in_specs=[row, row, _const_spec((1, D_MODEL))],
        out_specs=row,
        out_shape=jax.ShapeDtypeStruct((t, D_MODEL), F32),
        compiler_params=_cparams(("parallel",)),
        name="final",
    )(x2, peer, g)


def _rope_tables(seq):
    def cos_sin(pos, dim):
        inv = ROPE_THETA ** (-jnp.arange(0, dim, 2, dtype=F32) / dim)
        ang = pos.astype(F32)[:, None] * inv[None, :]
        ang = jnp.concatenate([ang, ang], axis=-1)
        return jnp.cos(ang), jnp.sin(ang)

    sign = jnp.where(jnp.arange(64) < 32, -1.0, 1.0).astype(F32)
    ca, sa = cos_sin(jnp.arange(seq), A_HD)
    rows = seq // GRID_W
    cr, sr = cos_sin(jnp.repeat(jnp.arange(rows), GRID_W), B_HD // 2)
    cc, sc = cos_sin(jnp.tile(jnp.arange(GRID_W), rows), B_HD // 2)
    cos_a = jnp.concatenate([ca, ca], axis=-1)
    sin_a = jnp.concatenate([sa * sign, sa * sign], axis=-1)
    cos_b = jnp.concatenate([cr, cc], axis=-1)
    sin_b = jnp.concatenate([sr * sign, sc * sign], axis=-1)
    return jnp.stack([cos_a, sin_a, cos_b, sin_b])


def _trunk(x, w, tb_proj=256, tb_peer=128):
    b, s, d = x.shape
    t = b * s
    x2d = x.reshape(t, d)
    cs = _rope_tables(s)
    qa, ka, va, qb, kb, vb1 = _inproj(x2d, w["g_mix"], w["w_in"], cs, w["qn"], w["kn"], s, min(tb_proj, s))
    sh = lambda a: a.reshape(b, s, a.shape[-1])
    oa = _window(w["sink"], sh(qa), sh(ka), sh(va))
    ob = _dense(sh(qb), sh(kb), sh(vb1))
    x2, h2, qry = _mixout(x2d, oa.reshape(t, A_Q), ob.reshape(t, B_Q), w["g_mix"], w["w_gate"], w["b_gate"],
                          w["w_oa"], w["w_ob"], w["w_out"], w["g_ffn"], w["w_query"], tb_proj)
    idx_t, gate_t = _topk(qry, w["keys"], tb_proj)
    peer = _peer(idx_t, gate_t, h2, w["tab_u"], w["tab_v"], tb_peer)
    y = _final(x2, peer, w["g_final"], tb_proj)
    return y.reshape(b, s, d)


def kernel(x_prompt, x_sample, g_mix, w_in, w_gate, b_gate, sink, q_norm_g, k_norm_g, w_oa, w_ob, w_out,
           g_ffn, w_query, sub_keys, expert_u, expert_v, g_final):
    assert w_in.shape[0] == 1, "single-layer trunk"
    l = 0
    w = {
        "g_mix": g_mix[l][None, :], "w_in": w_in[l].astype(BF16),
        "w_gate": w_gate[l].astype(BF16), "b_gate": b_gate[l][None, :],
        "sink": sink[l], "qn": q_norm_g[l][None, :], "kn": k_norm_g[l][None, :],
        "w_oa": w_oa[l].astype(BF16), "w_ob": w_ob[l].astype(BF16), "w_out": w_out[l].astype(BF16),
        "g_ffn": g_ffn[l][None, :], "w_query": w_query[l].astype(BF16),
        "keys": sub_keys[l].astype(BF16),
        "tab_u": _pack_table(expert_u[l]), "tab_v": _pack_table(expert_v[l]),
        "g_final": g_final[None, :],
    }
    return _trunk(x_prompt, w), _trunk(x_sample, w)
```

```python
import functools

import jax
import jax.numpy as jnp
from jax import lax
from jax.experimental import pallas as pl
from jax.experimental.pallas import tpu as pltpu

F32 = jnp.float32
BF16 = jnp.bfloat16
I32 = jnp.int32

D_MODEL = 1024
EPS = 1e-6
NEG_INF = -1e30
ROPE_THETA = 10000.0
GRID_W = 64
WINDOW = 128
A_HEADS, A_KV, A_HD = 16, 4, 64
B_HEADS, B_KV, B_HD = 8, 2, 128
A_Q, A_KVW = A_HEADS * A_HD, A_KV * A_HD
B_Q, B_KVW = B_HEADS * B_HD, B_KV * B_HD
IN_COLS = A_Q + 2 * A_KVW + B_Q + 2 * B_KVW
PK_HEADS, N_KEYS, PK_TOPK = 8, 128, 16
PK_DIM = 256
N_PICKS = PK_HEADS * PK_TOPK
LANES = 128
SUBLANES = 8
ROW_WORDS = D_MODEL // 2
ROW_SUB = ROW_WORDS // LANES
PICK_LANES = SUBLANES * N_PICKS
STAGE_TOKENS = 16
STAGE_BUFS = 4
LOG2E = 1.4426950408889634
VMEM_LIMIT = 56 * 1024 * 1024


def _cparams(sem):
    return pltpu.CompilerParams(dimension_semantics=sem, vmem_limit_bytes=VMEM_LIMIT)


def _const_spec(shape):
    nd = len(shape)
    return pl.BlockSpec(shape, lambda *_: (0,) * nd, pipeline_mode=pl.Buffered(1))


def _rms(x):
    return x * lax.rsqrt(jnp.mean(x * x, axis=-1, keepdims=True) + EPS)


def _inproj_kernel(x_ref, g_ref, w_ref, cs_ref, qn_ref, kn_ref,
                   qa_ref, ka_ref, va_ref, qb_ref, kb_ref, vb_ref):
    tb = x_ref.shape[0]
    hb = (_rms(x_ref[...]) * g_ref[...]).astype(BF16)
    lane = lax.broadcasted_iota(I32, (tb, LANES), 1)
    first = (lane % 64) < 32
    cos_a, sin_a, cos_b, sin_b = cs_ref[0], cs_ref[1], cs_ref[2], cs_ref[3]

    def rope(xc, cos, sin_signed):
        rot = jnp.where(first, pltpu.roll(xc, 96, 1), pltpu.roll(xc, 32, 1))
        return xc * cos + rot * sin_signed

    def proj(c0, width):
        return jnp.dot(hb, w_ref[:, c0:c0 + width], preferred_element_type=F32)

    for j in range(A_Q // 256):
        y = proj(j * 256, 256)
        for s in range(2):
            yc = rope(y[:, s * LANES:(s + 1) * LANES], cos_a, sin_a) * (A_HD ** -0.5)
            qa_ref[:, j * 256 + s * LANES:j * 256 + (s + 1) * LANES] = yc.astype(BF16)
    y = proj(A_Q, 256)
    for s in range(2):
        ka_ref[:, s * LANES:(s + 1) * LANES] = rope(y[:, s * LANES:(s + 1) * LANES], cos_a, sin_a).astype(BF16)
    va_ref[...] = proj(A_Q + A_KVW, 256).astype(BF16)
    base = A_Q + 2 * A_KVW
    for j in range(B_Q // 256):
        y = proj(base + j * 256, 256)
        for s in range(2):
            yc = _rms(y[:, s * LANES:(s + 1) * LANES]) * qn_ref[...]
            yc = rope(yc, cos_b, sin_b) * (B_HD ** -0.5 * LOG2E)
            qb_ref[:, j * 256 + s * LANES:j * 256 + (s + 1) * LANES] = yc.astype(BF16)
    y = proj(base + B_Q, 256)
    for s in range(2):
        yc = _rms(y[:, s * LANES:(s + 1) * LANES]) * kn_ref[...]
        kb_ref[:, s * LANES:(s + 1) * LANES] = rope(yc, cos_b, sin_b).astype(BF16)
    y = proj(base + B_Q + B_KVW, 256).astype(BF16)
    ones = jnp.ones((tb, B_HD), BF16)
    for kv in range(B_KV):
        vb_ref[:, 2 * kv * B_HD:(2 * kv + 1) * B_HD] = y[:, kv * B_HD:(kv + 1) * B_HD]
        vb_ref[:, (2 * kv + 1) * B_HD:(2 * kv + 2) * B_HD] = ones


def _inproj(x2d, g_mix, w_in, cs, qn, kn, seq, tb):
    t = x2d.shape[0]
    ns = seq // tb
    row = lambda w: pl.BlockSpec((tb, w), lambda i: (i, 0))
    out = lambda w: jax.ShapeDtypeStruct((t, w), BF16)
    return pl.pallas_call(
        _inproj_kernel,
        grid=(t // tb,),
        in_specs=[row(D_MODEL), _const_spec((1, D_MODEL)), _const_spec((D_MODEL, IN_COLS)),
                  pl.BlockSpec((4, tb, LANES), lambda i: (0, i % ns, 0)),
                  _const_spec((1, LANES)), _const_spec((1, LANES))],
        out_specs=[row(A_Q), row(A_KVW), row(A_KVW), row(B_Q), row(B_KVW), row(2 * B_KVW)],
        out_shape=[out(A_Q), out(A_KVW), out(A_KVW), out(B_Q), out(B_KVW), out(2 * B_KVW)],
        compiler_params=_cparams(("parallel",)),
        name="inproj",
    )(x2d, g_mix, w_in, cs, qn, kn)


def _window_kernel(sink_ref, q_ref, kp_ref, kc_ref, kn_ref, vp_ref, vc_ref, vn_ref, o_ref):
    i = pl.program_id(1)
    nb = pl.num_programs(1)
    blk = q_ref.shape[0]
    r = lax.broadcasted_iota(I32, (blk, 3 * blk), 0)
    c = lax.broadcasted_iota(I32, (blk, 3 * blk), 1)
    rel = c - blk - r
    lo_ok = jnp.where(i > 0, 0, blk)
    hi_ok = jnp.where(i < nb - 1, 3 * blk, 2 * blk)
    ok = (jnp.abs(rel) <= WINDOW) & (c >= lo_ok) & (c < hi_ok)
    k3 = jnp.concatenate([kp_ref[...], kc_ref[...], kn_ref[...]], axis=0)
    v3 = jnp.concatenate([vp_ref[...], vc_ref[...], vn_ref[...]], axis=0)
    group = A_HEADS // A_KV
    ss = []
    for h in range(A_HEADS):
        kv = h // group
        qh = q_ref[:, h * A_HD:(h + 1) * A_HD]
        kh = k3[:, kv * A_HD:(kv + 1) * A_HD]
        s = lax.dot_general(qh, kh, (((1,), (1,)), ((), ())), preferred_element_type=F32)
        ss.append(jnp.where(ok, s, NEG_INF))
    ms = [jnp.maximum(jnp.max(s, axis=1, keepdims=True), sink_ref[h]) for h, s in enumerate(ss)]
    ps = [jnp.exp(s - m) for s, m in zip(ss, ms)]
    dens = [jnp.sum(p, axis=1, keepdims=True) + jnp.exp(sink_ref[h] - m) for h, (p, m) in enumerate(zip(ps, ms))]
    outs = []
    for h in range(A_HEADS):
        kv = h // group
        vh = v3[:, kv * A_HD:(kv + 1) * A_HD]
        o = jnp.dot(ps[h].astype(BF16), vh, preferred_element_type=F32) / dens[h]
        outs.append(o.astype(BF16))
    o_ref[...] = jnp.concatenate(outs, axis=1)


def _window(sink, qa, ka, va, blk=128):
    b, s, _ = qa.shape
    nb = s // blk
    qspec = pl.BlockSpec((None, blk, A_Q), lambda bi, i: (bi, i, 0))
    prev = pl.BlockSpec((None, blk, A_KVW), lambda bi, i: (bi, jnp.maximum(i - 1, 0), 0))
    cur = pl.BlockSpec((None, blk, A_KVW), lambda bi, i: (bi, i, 0))
    nxt = pl.BlockSpec((None, blk, A_KVW), lambda bi, i: (bi, jnp.minimum(i + 1, nb - 1), 0))
    return pl.pallas_call(
        _window_kernel,
        grid=(b, nb),
        in_specs=[pl.BlockSpec(memory_space=pltpu.SMEM), qspec, prev, cur, nxt, prev, cur, nxt],
        out_specs=qspec,
        out_shape=jax.ShapeDtypeStruct((b, s, A_Q), BF16),
        compiler_params=_cparams(("parallel", "parallel")),
        name="window",
    )(sink, qa, ka, ka, ka, va, va, va)


def _dense_kernel(q_ref, k_ref, v_ref, o_ref, m_sc, acc_sc, *, tk, rb):
    tq = q_ref.shape[0]
    group = B_HEADS // B_KV
    qs = [q_ref[i * rb:(i + 1) * rb, g * B_HD:(g + 1) * B_HD] for g in range(group) for i in range(tq // rb)]
    m_sc[...] = jnp.full(m_sc.shape, -jnp.inf, F32)
    acc_sc[...] = jnp.zeros(acc_sc.shape, F32)

    def body(j, carry):
        off = pl.multiple_of(j * tk, tk)
        k = k_ref[pl.ds(off, tk), :]
        v = v_ref[pl.ds(off, tk), :]
        for n, q in enumerate(qs):
            rows = slice(n * rb, (n + 1) * rb)
            s = lax.dot_general(q, k, (((1,), (1,)), ((), ())), preferred_element_type=F32)
            m_prev = m_sc[rows]
            m_next = jnp.maximum(m_prev, jnp.max(s, axis=1, keepdims=True))
            p = jnp.exp2(s - jnp.concatenate([m_next] * (tk // LANES), axis=1))
            alpha = jnp.exp2(m_prev - m_next)
            acc_sc[rows] = jnp.concatenate([alpha, alpha], axis=1) * acc_sc[rows] + jnp.dot(
                p.astype(BF16), v, preferred_element_type=F32)
            m_sc[rows] = m_next
        return carry

    lax.fori_loop(0, k_ref.shape[0] // tk, body, 0)
    for g in range(group):
        acc = acc_sc[g * tq:(g + 1) * tq]
        o_ref[:, g * B_HD:(g + 1) * B_HD] = (acc[:, :B_HD] / acc[:, B_HD:]).astype(BF16)


def _dense(qb, kb, vb1, tq=1024, tk=1024, rb=256):
    b, s, _ = qb.shape
    tq, tk = min(tq, s), min(tk, s)
    rb = min(rb, tq)
    group = B_HEADS // B_KV
    gw = group * B_HD
    qspec = pl.BlockSpec((None, tq, gw), lambda bi, kv, i: (bi, i, kv))
    kspec = pl.BlockSpec((None, s, B_HD), lambda bi, kv, i: (bi, 0, kv))
    vspec = pl.BlockSpec((None, s, 2 * B_HD), lambda bi, kv, i: (bi, 0, kv))
    return pl.pallas_call(
        functools.partial(_dense_kernel, tk=tk, rb=rb),
        grid=(b, B_KV, s // tq),
        in_specs=[qspec, kspec, vspec],
        out_specs=qspec,
        out_shape=jax.ShapeDtypeStruct((b, s, B_Q), BF16),
        scratch_shapes=[pltpu.VMEM((group * tq, B_HD), F32), pltpu.VMEM((group * tq, 2 * B_HD), F32)],
        compiler_params=_cparams(("parallel", "parallel", "parallel")),
        name="dense",
    )(qb, kb, vb1)


def _mixout_kernel(x_ref, oa_ref, ob_ref, gmix_ref, wg_ref, bg_ref, woa_ref, wob_ref, wout_ref,
                   gffn_ref, wq_ref, x2_ref, h2_ref, qry_ref):
    x = x_ref[...]
    hb = (_rms(x) * gmix_ref[...]).astype(BF16)
    gates = jax.nn.sigmoid(jnp.dot(hb, wg_ref[...], preferred_element_type=F32) + bg_ref[...])
    ya = jnp.dot(oa_ref[...], woa_ref[...], preferred_element_type=F32)
    yb = jnp.dot(ob_ref[...], wob_ref[...], preferred_element_type=F32)
    merged = (gates[:, :D_MODEL] * ya + gates[:, D_MODEL:] * yb).astype(BF16)
    x2 = x + jnp.dot(merged, wout_ref[...], preferred_element_type=F32)
    x2_ref[...] = x2
    h2 = _rms(x2) * gffn_ref[...]
    h2_ref[...] = h2
    qry_ref[...] = jnp.dot(h2.astype(BF16), wq_ref[...], preferred_element_type=F32).astype(BF16)


def _mixout(x2d, oa, ob, gmix, wg, bg, woa, wob, wout, gffn, wq, tb):
    t = x2d.shape[0]
    row = lambda w: pl.BlockSpec((tb, w), lambda i: (i, 0))
    qw = PK_HEADS * PK_DIM
    return pl.pallas_call(
        _mixout_kernel,
        grid=(t // tb,),
        in_specs=[row(D_MODEL), row(A_Q), row(B_Q), _const_spec((1, D_MODEL)),
                  _const_spec((D_MODEL, 2 * D_MODEL)), _const_spec((1, 2 * D_MODEL)),
                  _const_spec((A_Q, D_MODEL)), _const_spec((B_Q, D_MODEL)), _const_spec((D_MODEL, D_MODEL)),
                  _const_spec((1, D_MODEL)), _const_spec((D_MODEL, qw))],
        out_specs=[row(D_MODEL), row(D_MODEL), row(qw)],
        out_shape=[jax.ShapeDtypeStruct((t, D_MODEL), F32), jax.ShapeDtypeStruct((t, D_MODEL), F32),
                   jax.ShapeDtypeStruct((t, qw), BF16)],
        compiler_params=_cparams(("parallel",)),
        name="mixout",
    )(x2d, oa, ob, gmix, wg, bg, woa, wob, wout, gffn, wq)


def _top_rows_multi(streams, k):
    big = float(PK_TOPK * PK_TOPK)
    vals = [v for v, _ in streams]
    ids = [i for _, i in streams]
    top_v = [[] for _ in streams]
    top_i = [[] for _ in streams]
    for _ in range(k):
        ms = [jnp.max(v, axis=0, keepdims=True) for v in vals]
        idxs = [jnp.min(jnp.where(v == m, i, big), axis=0, keepdims=True) for v, m, i in zip(vals, ms, ids)]
        vals = [jnp.where(i == idx, -jnp.inf, v) for v, i, idx in zip(vals, ids, idxs)]
        for n, (m, idx) in enumerate(zip(ms, idxs)):
            top_v[n].append(m)
            top_i[n].append(idx)
    return [(jnp.concatenate(v, axis=0), jnp.concatenate(i, axis=0)) for v, i in zip(top_v, top_i)]


def _top_rows(vals, ids, k):
    return _top_rows_multi([(vals, ids)], k)[0]


def _topk_kernel(qry_ref, keys_ref, idx_ref, gate_ref):
    tb = qry_ref.shape[0]
    row_id = lambda n: lax.broadcasted_iota(I32, (n, tb), 0).astype(F32)
    scores = []
    for p in range(2):
        q = qry_ref[:, p * N_KEYS:(p + 1) * N_KEYS]
        sc = lax.dot_general(keys_ref[0, p], q, (((1,), (1,)), ((), ())), preferred_element_type=F32)
        scores.append((sc, row_id(N_KEYS)))
    (s1, i1), (s2, i2) = _top_rows_multi(scores, PK_TOPK)
    half = PK_TOPK // 2
    cand = [s1[0:1] + s2]
    cpos = [row_id(PK_TOPK)]
    cidx = [i1[0:1] * float(N_KEYS) + i2]
    for a in range(1, half):
        cand.append(s1[a:a + 1] + s2[:half])
        cpos.append(row_id(half) + float(a * PK_TOPK))
        cidx.append(i1[a:a + 1] * float(N_KEYS) + i2[:half])
    cand.append(s1[half:] + s2[0:1])
    cpos.append((row_id(half) + float(half)) * float(PK_TOPK))
    cidx.append(i1[half:] * float(N_KEYS) + i2[0:1])
    cand, cpos, cidx = (jnp.concatenate(x, axis=0) for x in (cand, cpos, cidx))
    best, pos = _top_rows(cand, cpos, PK_TOPK)
    picked = [jnp.max(jnp.where(cpos == pos[j:j + 1], cidx, -1.0), axis=0, keepdims=True)
              for j in range(PK_TOPK)]
    idx_ref[...] = jnp.concatenate(picked, axis=0).astype(I32)
    e = jnp.exp(best - best[0:1])
    gate_ref[...] = e / jnp.sum(e, axis=0, keepdims=True)


def _topk(qry, keys, tb):
    t = qry.shape[0]
    return pl.pallas_call(
        _topk_kernel,
        grid=(t // tb, PK_HEADS),
        in_specs=[pl.BlockSpec((tb, PK_DIM), lambda i, h: (i, h)),
                  pl.BlockSpec((1, 2, N_KEYS, PK_DIM // 2), lambda i, h: (h, 0, 0, 0))],
        out_specs=[pl.BlockSpec((PK_TOPK, tb), lambda i, h: (h, i)),
                   pl.BlockSpec((PK_TOPK, tb), lambda i, h: (h, i))],
        out_shape=[jax.ShapeDtypeStruct((N_PICKS, t), I32), jax.ShapeDtypeStruct((N_PICKS, t), F32)],
        compiler_params=_cparams(("parallel", "parallel")),
        name="topk",
    )(qry, keys)


def _pack_table(w):
    bits = lax.bitcast_convert_type(w.astype(BF16), jnp.uint16).astype(jnp.uint32)
    bits = bits.reshape(w.shape[0], ROW_SUB, 2, LANES)
    packed = bits[:, :, 0] | (bits[:, :, 1] << 16)
    return lax.bitcast_convert_type(packed, I32).reshape(w.shape[0] * ROW_SUB, LANES)


def _split3(x):
    hi = x.astype(BF16).astype(F32)
    r1 = x - hi
    mid = r1.astype(BF16).astype(F32)
    lo = (r1 - mid).astype(BF16).astype(F32)
    return hi, mid, lo


def _load_table(tab_hbm, tab_vmem, sem):
    @pl.when(pl.program_id(0) == 0)
    def _():
        cp = pltpu.make_async_copy(tab_hbm, tab_vmem, sem)
        cp.start()
        cp.wait()


def _weights(stage, b, u, tab):
    rows = [tab[pl.ds(pl.multiple_of(stage[b, u * N_PICKS + k], ROW_SUB), ROW_SUB), :] for k in range(N_PICKS)]
    return pltpu.bitcast(jnp.concatenate(rows, axis=0), BF16)


def _staged_token_loop(rows_hbm, stage, sem, tb, token):
    step = pl.program_id(0)
    per_step = tb // STAGE_TOKENS
    total = pl.num_programs(0) * per_step
    ahead = STAGE_BUFS - 1

    def copy(g, b):
        return pltpu.make_async_copy(rows_hbm.at[g], stage.at[b], sem.at[b])

    @pl.when(step == 0)
    def _():
        for g in range(ahead):
            copy(g, g).start()

    def body(i, carry):
        for b in range(STAGE_BUFS):
            j = STAGE_BUFS * i + b
            g = step * per_step + j
            copy(g, b).wait()

            @pl.when(g + ahead < total)
            def _():
                copy(g + ahead, (b + ahead) % STAGE_BUFS).start()

            for u in range(STAGE_TOKENS):
                token(j * STAGE_TOKENS + u, b, u)
        return carry

    lax.fori_loop(0, per_step // STAGE_BUFS, body, 0)


def _stage_scratch():
    return [pltpu.SMEM((STAGE_BUFS, STAGE_TOKENS * N_PICKS), I32), pltpu.SemaphoreType.DMA((STAGE_BUFS,))]


def _lhs3(x):
    hi, mid, lo = _split3(x)
    return jnp.concatenate([hi, mid, lo, jnp.zeros_like(x)], axis=0).astype(BF16)


def _diag_mask():
    shape = (SUBLANES, PICK_LANES)
    return (lax.broadcasted_iota(I32, shape, 1) % SUBLANES) == lax.broadcasted_iota(I32, shape, 0)


def _group_allsum(x):
    n = x.shape[1]
    lane = lax.broadcasted_iota(I32, x.shape, 1)
    dist = 1
    while dist < SUBLANES:
        x = x + jnp.where((lane & dist) == 0, pltpu.roll(x, n - dist, 1), pltpu.roll(x, dist, 1))
        dist *= 2
    return x


def _peer_u_kernel(rows_hbm, h_ref, gate_ref, tab_hbm, c_ref, tab, dbuf, sem, stage, stage_sem):
    _load_table(tab_hbm, tab, sem)
    diag = _diag_mask()

    def token(t, b, u):
        w = _weights(stage, b, u, tab)
        r = lax.dot_general(_lhs3(h_ref[t]), w, (((1,), (1,)), ((), ())), preferred_element_type=F32)
        rc = r[0:8] + r[8:16] + r[16:24]
        dbuf[pl.ds(t, 1), :] = jnp.sum(jnp.where(diag, rc, 0.0), axis=0, keepdims=True)

    _staged_token_loop(rows_hbm, stage, stage_sem, h_ref.shape[0], token)
    d = _group_allsum(dbuf[...])
    shape = (N_PICKS, PICK_LANES)
    spread = (lax.broadcasted_iota(I32, shape, 1) // SUBLANES == lax.broadcasted_iota(I32, shape, 0)).astype(BF16)
    g8 = sum(jnp.dot(p.astype(BF16), spread, preferred_element_type=F32) for p in _split3(gate_ref[...]))
    c_ref[...] = g8 * (0.5 * d * (1.0 + lax.erf(d * (0.5 ** 0.5))))


def _peer_u(rows, h3, gate, tab, tb):
    t = h3.shape[0]
    return pl.pallas_call(
        _peer_u_kernel,
        grid=(t // tb,),
        in_specs=[pl.BlockSpec(memory_space=pl.ANY),
                  pl.BlockSpec((tb, SUBLANES, LANES), lambda i: (i, 0, 0)),
                  pl.BlockSpec((tb, N_PICKS), lambda i: (i, 0)),
                  pl.BlockSpec(memory_space=pl.ANY)],
        out_specs=pl.BlockSpec((tb, PICK_LANES), lambda i: (i, 0)),
        out_shape=jax.ShapeDtypeStruct((t, PICK_LANES), F32),
        scratch_shapes=[pltpu.VMEM(tab.shape, I32), pltpu.VMEM((tb, PICK_LANES), F32),
                        pltpu.SemaphoreType.DMA(())] + _stage_scratch(),
        compiler_params=_cparams(("arbitrary",)),
        name="peer_u",
    )(rows, h3, gate, tab)


def _peer_v_kernel(rows_hbm, c_ref, tab_hbm, out_ref, tab, sem, stage, stage_sem):
    _load_table(tab_hbm, tab, sem)
    diag = _diag_mask()

    def token(t, b, u):
        w = _weights(stage, b, u, tab)
        c8 = jnp.where(diag, c_ref[pl.ds(t, 1), :], 0.0)
        r = jnp.dot(_lhs3(c8), w, preferred_element_type=F32)
        out_ref[t] = r[0:8] + r[8:16] + r[16:24]

    _staged_token_loop(rows_hbm, stage, stage_sem, out_ref.shape[0], token)


def _peer_v(rows, c8, tab, tb):
    t = c8.shape[0]
    return pl.pallas_call(
        _peer_v_kernel,
        grid=(t // tb,),
        in_specs=[pl.BlockSpec(memory_space=pl.ANY),
                  pl.BlockSpec((tb, PICK_LANES), lambda i: (i, 0)),
                  pl.BlockSpec(memory_space=pl.ANY)],
        out_specs=pl.BlockSpec((tb, SUBLANES, LANES), lambda i: (i, 0, 0)),
        out_shape=jax.ShapeDtypeStruct((t, SUBLANES, LANES), F32),
        scratch_shapes=[pltpu.VMEM(tab.shape, I32), pltpu.SemaphoreType.DMA(())] + _stage_scratch(),
        compiler_params=_cparams(("arbitrary",)),
        name="peer_v",
    )(rows, c8, tab)


def _peer(idx_t, gate_t, h2, tab_u, tab_v, tb):
    t = h2.shape[0]
    assert tb % (STAGE_BUFS * STAGE_TOKENS) == 0 and t % tb == 0
    rows = (idx_t.T * ROW_SUB).reshape(t // STAGE_TOKENS, STAGE_TOKENS * N_PICKS)
    c8 = _peer_u(rows, h2.reshape(t, SUBLANES, LANES), gate_t.T, tab_u, tb)
    out = _peer_v(rows, c8, tab_v, tb)
    return out.reshape(t, D_MODEL)


def _final_kernel(x_ref, p_ref, g_ref, y_ref):
    y_ref[...] = _rms(x_ref[...] + p_ref[...]) * g_ref[...]


def _final(x2, peer, g, tb):
    t = x2.shape[0]
    row = pl.BlockSpec((tb, D_MODEL), lambda i: (i, 0))
    return pl.pallas_call(
        _final_kernel,
        grid=(t // tb,),
        in_specs=[row, row, _const_spec((1, D_MODEL))],
        out_specs=row,
        out_shape=jax.ShapeDtypeStruct((t, D_MODEL), F32),
        compiler_params=_cparams(("parallel",)),
        name="final",
    )(x2, peer, g)


def _rope_tables(seq):
    def cos_sin(pos, dim):
        inv = ROPE_THETA ** (-jnp.arange(0, dim, 2, dtype=F32) / dim)
        ang = pos.astype(F32)[:, None] * inv[None, :]
        ang = jnp.concatenate([ang, ang], axis=-1)
        return jnp.cos(ang), jnp.sin(ang)

    sign = jnp.where(jnp.arange(64) < 32, -1.0, 1.0).astype(F32)
    ca, sa = cos_sin(jnp.arange(seq), A_HD)
    rows = seq // GRID_W
    cr, sr = cos_sin(jnp.repeat(jnp.arange(rows), GRID_W), B_HD // 2)
    cc, sc = cos_sin(jnp.tile(jnp.arange(GRID_W), rows), B_HD // 2)
    cos_a = jnp.concatenate([ca, ca], axis=-1)
    sin_a = jnp.concatenate([sa * sign, sa * sign], axis=-1)
    cos_b = jnp.concatenate([cr, cc], axis=-1)
    sin_b = jnp.concatenate([sr * sign, sc * sign], axis=-1)
    return jnp.stack([cos_a, sin_a, cos_b, sin_b])


def _trunk(x, w, tb_proj=256, tb_peer=128):
    b, s, d = x.shape
    t = b * s
    x2d = x.reshape(t, d)
    cs = _rope_tables(s)
    qa, ka, va, qb, kb, vb1 = _inproj(x2d, w["g_mix"], w["w_in"], cs, w["qn"], w["kn"], s, min(tb_proj, s))
    sh = lambda a: a.reshape(b, s, a.shape[-1])
    oa = _window(w["sink"], sh(qa), sh(ka), sh(va))
    ob = _dense(sh(qb), sh(kb), sh(vb1))
    x2, h2, qry = _mixout(x2d, oa.reshape(t, A_Q), ob.reshape(t, B_Q), w["g_mix"], w["w_gate"], w["b_gate"],
                          w["w_oa"], w["w_ob"], w["w_out"], w["g_ffn"], w["w_query"], tb_proj)
    idx_t, gate_t = _topk(qry, w["keys"], tb_proj)
    peer = _peer(idx_t, gate_t, h2, w["tab_u"], w["tab_v"], tb_peer)
    y = _final(x2, peer, w["g_final"], tb_proj)
    return y.reshape(b, s, d)


def kernel(x_prompt, x_sample, g_mix, w_in, w_gate, b_gate, sink, q_norm_g, k_norm_g, w_oa, w_ob, w_out,
           g_ffn, w_query, sub_keys, expert_u, expert_v, g_final):
    assert w_in.shape[0] == 1, "single-layer trunk"
    l = 0
    w = {
        "g_mix": g_mix[l][None, :], "w_in": w_in[l].astype(BF16),
        "w_gate": w_gate[l].astype(BF16), "b_gate": b_gate[l][None, :],
        "sink": sink[l], "qn": q_norm_g[l][None, :], "kn": k_norm_g[l][None, :],
        "w_oa": w_oa[l].astype(BF16), "w_ob": w_ob[l].astype(BF16), "w_out": w_out[l].astype(BF16),
        "g_ffn": g_ffn[l][None, :], "w_query": w_query[l].astype(BF16),
        "keys": sub_keys[l].astype(BF16),
        "tab_u": _pack_table(expert_u[l]), "tab_v": _pack_table(expert_v[l]),
        "g_final": g_final[None, :],
    }
    return _trunk(x_prompt, w), _trunk(x_sample, w)
```

```python
import functools

import jax
import jax.numpy as jnp
from jax import lax
from jax.experimental import pallas as pl
from jax.experimental.pallas import tpu as pltpu

F32 = jnp.float32
BF16 = jnp.bfloat16
I32 = jnp.int32

D_MODEL = 1024
EPS = 1e-6
NEG_INF = -1e30
ROPE_THETA = 10000.0
GRID_W = 64
WINDOW = 128
A_HEADS, A_KV, A_HD = 16, 4, 64
B_HEADS, B_KV, B_HD = 8, 2, 128
A_Q, A_KVW = A_HEADS * A_HD, A_KV * A_HD
B_Q, B_KVW = B_HEADS * B_HD, B_KV * B_HD
IN_COLS = A_Q + 2 * A_KVW + B_Q + 2 * B_KVW
PK_HEADS, N_KEYS, PK_TOPK = 8, 128, 16
PK_DIM = 256
N_PICKS = PK_HEADS * PK_TOPK
LANES = 128
SUBLANES = 8
ROW_WORDS = D_MODEL // 2
ROW_SUB = ROW_WORDS // LANES
PICK_LANES = SUBLANES * N_PICKS
STAGE_TOKENS = 32
STAGE_BUFS = 2
LOG2E = 1.4426950408889634
VMEM_LIMIT = 56 * 1024 * 1024


def _cparams(sem):
    return pltpu.CompilerParams(dimension_semantics=sem, vmem_limit_bytes=VMEM_LIMIT)


def _const_spec(shape):
    nd = len(shape)
    return pl.BlockSpec(shape, lambda *_: (0,) * nd, pipeline_mode=pl.Buffered(1))


def _rms(x):
    return x * lax.rsqrt(jnp.mean(x * x, axis=-1, keepdims=True) + EPS)


def _inproj_kernel(x_ref, g_ref, w_ref, cs_ref, qn_ref, kn_ref,
                   qa_ref, ka_ref, va_ref, qb_ref, kb_ref, vb_ref):
    tb = x_ref.shape[0]
    hb = (_rms(x_ref[...]) * g_ref[...]).astype(BF16)
    lane = lax.broadcasted_iota(I32, (tb, LANES), 1)
    first = (lane % 64) < 32
    cos_a, sin_a, cos_b, sin_b = cs_ref[0], cs_ref[1], cs_ref[2], cs_ref[3]

    def rope(xc, cos, sin_signed):
        rot = jnp.where(first, pltpu.roll(xc, 96, 1), pltpu.roll(xc, 32, 1))
        return xc * cos + rot * sin_signed

    def proj(c0, width):
        return jnp.dot(hb, w_ref[:, c0:c0 + width], preferred_element_type=F32)

    for j in range(A_Q // 256):
        y = proj(j * 256, 256)
        for s in range(2):
            yc = rope(y[:, s * LANES:(s + 1) * LANES], cos_a, sin_a) * (A_HD ** -0.5)
            qa_ref[:, j * 256 + s * LANES:j * 256 + (s + 1) * LANES] = yc.astype(BF16)
    y = proj(A_Q, 256)
    for s in range(2):
        ka_ref[:, s * LANES:(s + 1) * LANES] = rope(y[:, s * LANES:(s + 1) * LANES], cos_a, sin_a).astype(BF16)
    va_ref[...] = proj(A_Q + A_KVW, 256).astype(BF16)
    base = A_Q + 2 * A_KVW
    for j in range(B_Q // 256):
        y = proj(base + j * 256, 256)
        for s in range(2):
            yc = _rms(y[:, s * LANES:(s + 1) * LANES]) * qn_ref[...]
            yc = rope(yc, cos_b, sin_b) * (B_HD ** -0.5 * LOG2E)
            qb_ref[:, j * 256 + s * LANES:j * 256 + (s + 1) * LANES] = yc.astype(BF16)
    y = proj(base + B_Q, 256)
    for s in range(2):
        yc = _rms(y[:, s * LANES:(s + 1) * LANES]) * kn_ref[...]
        kb_ref[:, s * LANES:(s + 1) * LANES] = rope(yc, cos_b, sin_b).astype(BF16)
    y = proj(base + B_Q + B_KVW, 256).astype(BF16)
    ones = jnp.ones((tb, B_HD), BF16)
    for kv in range(B_KV):
        vb_ref[:, 2 * kv * B_HD:(2 * kv + 1) * B_HD] = y[:, kv * B_HD:(kv + 1) * B_HD]
        vb_ref[:, (2 * kv + 1) * B_HD:(2 * kv + 2) * B_HD] = ones


def _inproj(x2d, g_mix, w_in, cs, qn, kn, seq, tb):
    t = x2d.shape[0]
    ns = seq // tb
    row = lambda w: pl.BlockSpec((tb, w), lambda i: (i, 0))
    out = lambda w: jax.ShapeDtypeStruct((t, w), BF16)
    return pl.pallas_call(
        _inproj_kernel,
        grid=(t // tb,),
        in_specs=[row(D_MODEL), _const_spec((1, D_MODEL)), _const_spec((D_MODEL, IN_COLS)),
                  pl.BlockSpec((4, tb, LANES), lambda i: (0, i % ns, 0)),
                  _const_spec((1, LANES)), _const_spec((1, LANES))],
        out_specs=[row(A_Q), row(A_KVW), row(A_KVW), row(B_Q), row(B_KVW), row(2 * B_KVW)],
        out_shape=[out(A_Q), out(A_KVW), out(A_KVW), out(B_Q), out(B_KVW), out(2 * B_KVW)],
        compiler_params=_cparams(("parallel",)),
        name="inproj",
    )(x2d, g_mix, w_in, cs, qn, kn)


def _window_kernel(sink_ref, q_ref, kp_ref, kc_ref, kn_ref, vp_ref, vc_ref, vn_ref, o_ref):
    i = pl.program_id(1)
    nb = pl.num_programs(1)
    blk = q_ref.shape[0]
    r = lax.broadcasted_iota(I32, (blk, 3 * blk), 0)
    c = lax.broadcasted_iota(I32, (blk, 3 * blk), 1)
    rel = c - blk - r
    lo_ok = jnp.where(i > 0, 0, blk)
    hi_ok = jnp.where(i < nb - 1, 3 * blk, 2 * blk)
    ok = (jnp.abs(rel) <= WINDOW) & (c >= lo_ok) & (c < hi_ok)
    k3 = jnp.concatenate([kp_ref[...], kc_ref[...], kn_ref[...]], axis=0)
    v3 = jnp.concatenate([vp_ref[...], vc_ref[...], vn_ref[...]], axis=0)
    group = A_HEADS // A_KV
    ss = []
    for h in range(A_HEADS):
        kv = h // group
        qh = q_ref[:, h * A_HD:(h + 1) * A_HD]
        kh = k3[:, kv * A_HD:(kv + 1) * A_HD]
        s = lax.dot_general(qh, kh, (((1,), (1,)), ((), ())), preferred_element_type=F32)
        ss.append(jnp.where(ok, s, NEG_INF))
    ms = [jnp.maximum(jnp.max(s, axis=1, keepdims=True), sink_ref[h]) for h, s in enumerate(ss)]
    ps = [jnp.exp(s - m) for s, m in zip(ss, ms)]
    dens = [jnp.sum(p, axis=1, keepdims=True) + jnp.exp(sink_ref[h] - m) for h, (p, m) in enumerate(zip(ps, ms))]
    outs = []
    for h in range(A_HEADS):
        kv = h // group
        vh = v3[:, kv * A_HD:(kv + 1) * A_HD]
        o = jnp.dot(ps[h].astype(BF16), vh, preferred_element_type=F32) / dens[h]
        outs.append(o.astype(BF16))
    o_ref[...] = jnp.concatenate(outs, axis=1)


def _window(sink, qa, ka, va, blk=128):
    b, s, _ = qa.shape
    nb = s // blk
    qspec = pl.BlockSpec((None, blk, A_Q), lambda bi, i: (bi, i, 0))
    prev = pl.BlockSpec((None, blk, A_KVW), lambda bi, i: (bi, jnp.maximum(i - 1, 0), 0))
    cur = pl.BlockSpec((None, blk, A_KVW), lambda bi, i: (bi, i, 0))
    nxt = pl.BlockSpec((None, blk, A_KVW), lambda bi, i: (bi, jnp.minimum(i + 1, nb - 1), 0))
    return pl.pallas_call(
        _window_kernel,
        grid=(b, nb),
        in_specs=[pl.BlockSpec(memory_space=pltpu.SMEM), qspec, prev, cur, nxt, prev, cur, nxt],
        out_specs=qspec,
        out_shape=jax.ShapeDtypeStruct((b, s, A_Q), BF16),
        compiler_params=_cparams(("parallel", "parallel")),
        name="window",
    )(sink, qa, ka, ka, ka, va, va, va)


def _dense_kernel(q_ref, k_ref, v_ref, o_ref, m_sc, acc_sc, *, tk, rb):
    tq = q_ref.shape[0]
    group = B_HEADS // B_KV
    qs = [q_ref[i * rb:(i + 1) * rb, g * B_HD:(g + 1) * B_HD] for g in range(group) for i in range(tq // rb)]
    m_sc[...] = jnp.full(m_sc.shape, -jnp.inf, F32)
    acc_sc[...] = jnp.zeros(acc_sc.shape, F32)

    def body(j, carry):
        off = pl.multiple_of(j * tk, tk)
        k = k_ref[pl.ds(off, tk), :]
        v = v_ref[pl.ds(off, tk), :]
        for n, q in enumerate(qs):
            rows = slice(n * rb, (n + 1) * rb)
            s = lax.dot_general(q, k, (((1,), (1,)), ((), ())), preferred_element_type=F32)
            m_prev = m_sc[rows]
            m_next = jnp.maximum(m_prev, jnp.max(s, axis=1, keepdims=True))
            p = jnp.exp2(s - jnp.concatenate([m_next] * (tk // LANES), axis=1))
            alpha = jnp.exp2(m_prev - m_next)
            acc_sc[rows] = jnp.concatenate([alpha, alpha], axis=1) * acc_sc[rows] + jnp.dot(
                p.astype(BF16), v, preferred_element_type=F32)
            m_sc[rows] = m_next
        return carry

    lax.fori_loop(0, k_ref.shape[0] // tk, body, 0)
    for g in range(group):
        acc = acc_sc[g * tq:(g + 1) * tq]
        o_ref[:, g * B_HD:(g + 1) * B_HD] = (acc[:, :B_HD] / acc[:, B_HD:]).astype(BF16)


def _dense(qb, kb, vb1, tq=1024, tk=2048, rb=256):
    b, s, _ = qb.shape
    tq, tk = min(tq, s), min(tk, s)
    rb = min(rb, tq)
    group = B_HEADS // B_KV
    gw = group * B_HD
    qspec = pl.BlockSpec((None, tq, gw), lambda bi, kv, i: (bi, i, kv))
    kspec = pl.BlockSpec((None, s, B_HD), lambda bi, kv, i: (bi, 0, kv))
    vspec = pl.BlockSpec((None, s, 2 * B_HD), lambda bi, kv, i: (bi, 0, kv))
    return pl.pallas_call(
        functools.partial(_dense_kernel, tk=tk, rb=rb),
        grid=(b, B_KV, s // tq),
        in_specs=[qspec, kspec, vspec],
        out_specs=qspec,
        out_shape=jax.ShapeDtypeStruct((b, s, B_Q), BF16),
        scratch_shapes=[pltpu.VMEM((group * tq, B_HD), F32), pltpu.VMEM((group * tq, 2 * B_HD), F32)],
        compiler_params=_cparams(("parallel", "parallel", "parallel")),
        name="dense",
    )(qb, kb, vb1)


def _mixout_kernel(x_ref, oa_ref, ob_ref, gmix_ref, wg_ref, bg_ref, woa_ref, wob_ref, wout_ref,
                   gffn_ref, wq_ref, x2_ref, h2_ref, qry_ref):
    x = x_ref[...]
    hb = (_rms(x) * gmix_ref[...]).astype(BF16)
    gates = jax.nn.sigmoid(jnp.dot(hb, wg_ref[...], preferred_element_type=F32) + bg_ref[...])
    ya = jnp.dot(oa_ref[...], woa_ref[...], preferred_element_type=F32)
    yb = jnp.dot(ob_ref[...], wob_ref[...], preferred_element_type=F32)
    merged = (gates[:, :D_MODEL] * ya + gates[:, D_MODEL:] * yb).astype(BF16)
    x2 = x + jnp.dot(merged, wout_ref[...], preferred_element_type=F32)
    x2_ref[...] = x2
    h2 = _rms(x2) * gffn_ref[...]
    h2_ref[...] = h2
    qry_ref[...] = jnp.dot(h2.astype(BF16), wq_ref[...], preferred_element_type=F32).astype(BF16)


def _mixout(x2d, oa, ob, gmix, wg, bg, woa, wob, wout, gffn, wq, tb):
    t = x2d.shape[0]
    row = lambda w: pl.BlockSpec((tb, w), lambda i: (i, 0))
    qw = PK_HEADS * PK_DIM
    return pl.pallas_call(
        _mixout_kernel,
        grid=(t // tb,),
        in_specs=[row(D_MODEL), row(A_Q), row(B_Q), _const_spec((1, D_MODEL)),
                  _const_spec((D_MODEL, 2 * D_MODEL)), _const_spec((1, 2 * D_MODEL)),
                  _const_spec((A_Q, D_MODEL)), _const_spec((B_Q, D_MODEL)), _const_spec((D_MODEL, D_MODEL)),
                  _const_spec((1, D_MODEL)), _const_spec((D_MODEL, qw))],
        out_specs=[row(D_MODEL), row(D_MODEL), row(qw)],
        out_shape=[jax.ShapeDtypeStruct((t, D_MODEL), F32), jax.ShapeDtypeStruct((t, D_MODEL), F32),
                   jax.ShapeDtypeStruct((t, qw), BF16)],
        compiler_params=_cparams(("parallel",)),
        name="mixout",
    )(x2d, oa, ob, gmix, wg, bg, woa, wob, wout, gffn, wq)


def _top_rows_multi(streams, k):
    big = float(PK_TOPK * PK_TOPK)
    vals = [v for v, _ in streams]
    ids = [i for _, i in streams]
    top_v = [[] for _ in streams]
    top_i = [[] for _ in streams]
    for _ in range(k):
        ms = [jnp.max(v, axis=0, keepdims=True) for v in vals]
        idxs = [jnp.min(jnp.where(v == m, i, big), axis=0, keepdims=True) for v, m, i in zip(vals, ms, ids)]
        vals = [jnp.where(i == idx, -jnp.inf, v) for v, i, idx in zip(vals, ids, idxs)]
        for n, (m, idx) in enumerate(zip(ms, idxs)):
            top_v[n].append(m)
            top_i[n].append(idx)
    return [(jnp.concatenate(v, axis=0), jnp.concatenate(i, axis=0)) for v, i in zip(top_v, top_i)]


def _top_rows(vals, ids, k):
    return _top_rows_multi([(vals, ids)], k)[0]


def _topk_kernel(qry_ref, keys_ref, idx_ref, gate_ref):
    tb = qry_ref.shape[0]
    row_id = lambda n: lax.broadcasted_iota(I32, (n, tb), 0).astype(F32)
    scores = []
    for p in range(2):
        q = qry_ref[:, p * N_KEYS:(p + 1) * N_KEYS]
        sc = lax.dot_general(keys_ref[0, p], q, (((1,), (1,)), ((), ())), preferred_element_type=F32)
        scores.append((sc, row_id(N_KEYS)))
    (s1, i1), (s2, i2) = _top_rows_multi(scores, PK_TOPK)
    half = PK_TOPK // 2
    cand = [s1[0:1] + s2]
    cpos = [row_id(PK_TOPK)]
    cidx = [i1[0:1] * float(N_KEYS) + i2]
    for a in range(1, half):
        cand.append(s1[a:a + 1] + s2[:half])
        cpos.append(row_id(half) + float(a * PK_TOPK))
        cidx.append(i1[a:a + 1] * float(N_KEYS) + i2[:half])
    cand.append(s1[half:] + s2[0:1])
    cpos.append((row_id(half) + float(half)) * float(PK_TOPK))
    cidx.append(i1[half:] * float(N_KEYS) + i2[0:1])
    cand, cpos, cidx = (jnp.concatenate(x, axis=0) for x in (cand, cpos, cidx))
    best, pos = _top_rows(cand, cpos, PK_TOPK)
    picked = [jnp.max(jnp.where(cpos == pos[j:j + 1], cidx, -1.0), axis=0, keepdims=True)
              for j in range(PK_TOPK)]
    idx_ref[...] = jnp.concatenate(picked, axis=0).astype(I32)
    e = jnp.exp(best - best[0:1])
    gate_ref[...] = e / jnp.sum(e, axis=0, keepdims=True)


def _topk(qry, keys, tb):
    t = qry.shape[0]
    return pl.pallas_call(
        _topk_kernel,
        grid=(t // tb, PK_HEADS),
        in_specs=[pl.BlockSpec((tb, PK_DIM), lambda i, h: (i, h)),
                  pl.BlockSpec((1, 2, N_KEYS, PK_DIM // 2), lambda i, h: (h, 0, 0, 0))],
        out_specs=[pl.BlockSpec((PK_TOPK, tb), lambda i, h: (h, i)),
                   pl.BlockSpec((PK_TOPK, tb), lambda i, h: (h, i))],
        out_shape=[jax.ShapeDtypeStruct((N_PICKS, t), I32), jax.ShapeDtypeStruct((N_PICKS, t), F32)],
        compiler_params=_cparams(("parallel", "parallel")),
        name="topk",
    )(qry, keys)


def _pack_table(w):
    bits = lax.bitcast_convert_type(w.astype(BF16), jnp.uint16).astype(jnp.uint32)
    bits = bits.reshape(w.shape[0], ROW_SUB, 2, LANES)
    packed = bits[:, :, 0] | (bits[:, :, 1] << 16)
    return lax.bitcast_convert_type(packed, I32).reshape(w.shape[0] * ROW_SUB, LANES)


def _split3(x):
    hi = x.astype(BF16).astype(F32)
    r1 = x - hi
    mid = r1.astype(BF16).astype(F32)
    lo = (r1 - mid).astype(BF16).astype(F32)
    return hi, mid, lo


def _load_table(tab_hbm, tab_vmem, sem):
    @pl.when(pl.program_id(0) == 0)
    def _():
        cp = pltpu.make_async_copy(tab_hbm, tab_vmem, sem)
        cp.start()
        cp.wait()


def _weights(stage, b, u, tab):
    rows = [tab[pl.ds(pl.multiple_of(stage[b, u * N_PICKS + k], ROW_SUB), ROW_SUB), :] for k in range(N_PICKS)]
    return pltpu.bitcast(jnp.concatenate(rows, axis=0), BF16)


def _staged_token_loop(rows_hbm, stage, sem, tb, token):
    step = pl.program_id(0)
    per_step = tb // STAGE_TOKENS
    total = pl.num_programs(0) * per_step
    ahead = STAGE_BUFS - 1

    def copy(g, b):
        return pltpu.make_async_copy(rows_hbm.at[g], stage.at[b], sem.at[b])

    @pl.when(step == 0)
    def _():
        for g in range(ahead):
            copy(g, g).start()

    def body(i, carry):
        for b in range(STAGE_BUFS):
            j = STAGE_BUFS * i + b
            g = step * per_step + j
            copy(g, b).wait()

            @pl.when(g + ahead < total)
            def _():
                copy(g + ahead, (b + ahead) % STAGE_BUFS).start()

            for u in range(STAGE_TOKENS):
                token(j * STAGE_TOKENS + u, b, u)
        return carry

    lax.fori_loop(0, per_step // STAGE_BUFS, body, 0)


def _stage_scratch():
    return [pltpu.SMEM((STAGE_BUFS, STAGE_TOKENS * N_PICKS), I32), pltpu.SemaphoreType.DMA((STAGE_BUFS,))]


def _lhs3(x):
    hi, mid, lo = _split3(x)
    return jnp.concatenate([hi, mid, lo, jnp.zeros_like(x)], axis=0).astype(BF16)


def _diag_mask():
    shape = (SUBLANES, PICK_LANES)
    return (lax.broadcasted_iota(I32, shape, 1) % SUBLANES) == lax.broadcasted_iota(I32, shape, 0)


def _group_allsum(x):
    n = x.shape[1]
    lane = lax.broadcasted_iota(I32, x.shape, 1)
    dist = 1
    while dist < SUBLANES:
        x = x + jnp.where((lane & dist) == 0, pltpu.roll(x, n - dist, 1), pltpu.roll(x, dist, 1))
        dist *= 2
    return x


def _peer_u_kernel(rows_hbm, h_ref, gate_ref, tab_hbm, c_ref, tab, dbuf, sem, stage, stage_sem):
    _load_table(tab_hbm, tab, sem)
    diag = _diag_mask()

    def token(t, b, u):
        w = _weights(stage, b, u, tab)
        r = lax.dot_general(_lhs3(h_ref[t]), w, (((1,), (1,)), ((), ())), preferred_element_type=F32)
        rc = r[0:8] + r[8:16] + r[16:24]
        dbuf[pl.ds(t, 1), :] = jnp.sum(jnp.where(diag, rc, 0.0), axis=0, keepdims=True)

    _staged_token_loop(rows_hbm, stage, stage_sem, h_ref.shape[0], token)
    d = _group_allsum(dbuf[...])
    shape = (N_PICKS, PICK_LANES)
    spread = (lax.broadcasted_iota(I32, shape, 1) // SUBLANES == lax.broadcasted_iota(I32, shape, 0)).astype(BF16)
    g8 = sum(jnp.dot(p.astype(BF16), spread, preferred_element_type=F32) for p in _split3(gate_ref[...]))
    c_ref[...] = g8 * (0.5 * d * (1.0 + lax.erf(d * (0.5 ** 0.5))))


def _peer_u(rows, h3, gate, tab, tb):
    t = h3.shape[0]
    return pl.pallas_call(
        _peer_u_kernel,
        grid=(t // tb,),
        in_specs=[pl.BlockSpec(memory_space=pl.ANY),
                  pl.BlockSpec((tb, SUBLANES, LANES), lambda i: (i, 0, 0)),
                  pl.BlockSpec((tb, N_PICKS), lambda i: (i, 0)),
                  pl.BlockSpec(memory_space=pl.ANY)],
        out_specs=pl.BlockSpec((tb, PICK_LANES), lambda i: (i, 0)),
        out_shape=jax.ShapeDtypeStruct((t, PICK_LANES), F32),
        scratch_shapes=[pltpu.VMEM(tab.shape, I32), pltpu.VMEM((tb, PICK_LANES), F32),
                        pltpu.SemaphoreType.DMA(())] + _stage_scratch(),
        compiler_params=_cparams(("arbitrary",)),
        name="peer_u",
    )(rows, h3, gate, tab)


def _peer_v_kernel(rows_hbm, c_ref, tab_hbm, out_ref, tab, sem, stage, stage_sem):
    _load_table(tab_hbm, tab, sem)
    diag = _diag_mask()

    def token(t, b, u):
        w = _weights(stage, b, u, tab)
        c8 = jnp.where(diag, c_ref[pl.ds(t, 1), :], 0.0)
        r = jnp.dot(_lhs3(c8), w, preferred_element_type=F32)
        out_ref[t] = r[0:8] + r[8:16] + r[16:24]

    _staged_token_loop(rows_hbm, stage, stage_sem, out_ref.shape[0], token)


def _peer_v(rows, c8, tab, tb):
    t = c8.shape[0]
    return pl.pallas_call(
        _peer_v_kernel,
        grid=(t // tb,),
        in_specs=[pl.BlockSpec(memory_space=pl.ANY),
                  pl.BlockSpec((tb, PICK_LANES), lambda i: (i, 0)),
                  pl.BlockSpec(memory_space=pl.ANY)],
        out_specs=pl.BlockSpec((tb, SUBLANES, LANES), lambda i: (i, 0, 0)),
        out_shape=jax.ShapeDtypeStruct((t, SUBLANES, LANES), F32),
        scratch_shapes=[pltpu.VMEM(tab.shape, I32), pltpu.SemaphoreType.DMA(())] + _stage_scratch(),
        compiler_params=_cparams(("arbitrary",)),
        name="peer_v",
    )(rows, c8, tab)


def _peer(idx_t, gate_t, h2, tab_u, tab_v, tb):
    t = h2.shape[0]
    assert tb % (STAGE_BUFS * STAGE_TOKENS) == 0 and t % tb == 0
    rows = (idx_t.T * ROW_SUB).reshape(t // STAGE_TOKENS, STAGE_TOKENS * N_PICKS)
    c8 = _peer_u(rows, h2.reshape(t, SUBLANES, LANES), gate_t.T, tab_u, tb)
    out = _peer_v(rows, c8, tab_v, tb)
    return out.reshape(t, D_MODEL)


def _final_kernel(x_ref, p_ref, g_ref, y_ref):
    y_ref[...] = _rms(x_ref[...] + p_ref[...]) * g_ref[...]


def _final(x2, peer, g, tb):
    t = x2.shape[0]
    row = pl.BlockSpec((tb, D_MODEL), lambda i: (i, 0))
    return pl.pallas_call(
        _final_kernel,
        grid=(t // tb,),
        in_specs=[row, row, _const_spec((1, D_MODEL))],
        out_specs=row,
        out_shape=jax.ShapeDtypeStruct((t, D_MODEL), F32),
        compiler_params=_cparams(("parallel",)),
        name="final",
    )(x2, peer, g)


def _rope_tables(seq):
    def cos_sin(pos, dim):
        inv = ROPE_THETA ** (-jnp.arange(0, dim, 2, dtype=F32) / dim)
        ang = pos.astype(F32)[:, None] * inv[None, :]
        ang = jnp.concatenate([ang, ang], axis=-1)
        return jnp.cos(ang), jnp.sin(ang)

    sign = jnp.where(jnp.arange(64) < 32, -1.0, 1.0).astype(F32)
    ca, sa = cos_sin(jnp.arange(seq), A_HD)
    rows = seq // GRID_W
    cr, sr = cos_sin(jnp.repeat(jnp.arange(rows), GRID_W), B_HD // 2)
    cc, sc = cos_sin(jnp.tile(jnp.arange(GRID_W), rows), B_HD // 2)
    cos_a = jnp.concatenate([ca, ca], axis=-1)
    sin_a = jnp.concatenate([sa * sign, sa * sign], axis=-1)
    cos_b = jnp.concatenate([cr, cc], axis=-1)
    sin_b = jnp.concatenate([sr * sign, sc * sign], axis=-1)
    return jnp.stack([cos_a, sin_a, cos_b, sin_b])


def _trunk(x, w, tb_proj=256, tb_peer=128):
    b, s, d = x.shape
    t = b * s
    x2d = x.reshape(t, d)
    cs = _rope_tables(s)
    qa, ka, va, qb, kb, vb1 = _inproj(x2d, w["g_mix"], w["w_in"], cs, w["qn"], w["kn"], s, min(tb_proj, s))
    sh = lambda a: a.reshape(b, s, a.shape[-1])
    oa = _window(w["sink"], sh(qa), sh(ka), sh(va))
    ob = _dense(sh(qb), sh(kb), sh(vb1))
    x2, h2, qry = _mixout(x2d, oa.reshape(t, A_Q), ob.reshape(t, B_Q), w["g_mix"], w["w_gate"], w["b_gate"],
                          w["w_oa"], w["w_ob"], w["w_out"], w["g_ffn"], w["w_query"], tb_proj)
    idx_t, gate_t = _topk(qry, w["keys"], tb_proj)
    peer = _peer(idx_t, gate_t, h2, w["tab_u"], w["tab_v"], tb_peer)
    y = _final(x2, peer, w["g_final"], tb_proj)
    return y.reshape(b, s, d)


def kernel(x_prompt, x_sample, g_mix, w_in, w_gate, b_gate, sink, q_norm_g, k_norm_g, w_oa, w_ob, w_out,
           g_ffn, w_query, sub_keys, expert_u, expert_v, g_final):
    assert w_in.shape[0] == 1, "single-layer trunk"
    l = 0
    w = {
        "g_mix": g_mix[l][None, :], "w_in": w_in[l].astype(BF16),
        "w_gate": w_gate[l].astype(BF16), "b_gate": b_gate[l][None, :],
        "sink": sink[l], "qn": q_norm_g[l][None, :], "kn": k_norm_g[l][None, :],
        "w_oa": w_oa[l].astype(BF16), "w_ob": w_ob[l].astype(BF16), "w_out": w_out[l].astype(BF16),
        "g_ffn": g_ffn[l][None, :], "w_query": w_query[l].astype(BF16),
        "keys": sub_keys[l].astype(BF16),
        "tab_u": _pack_table(expert_u[l]), "tab_v": _pack_table(expert_v[l]),
        "g_final": g_final[None, :],
    }
    return _trunk(x_prompt, w), _trunk(x_sample, w)
```

```python
import functools

import jax
import jax.numpy as jnp
from jax import lax
from jax.experimental import pallas as pl
from jax.experimental.pallas import tpu as pltpu

F32 = jnp.float32
BF16 = jnp.bfloat16
I32 = jnp.int32

D_MODEL = 1024
EPS = 1e-6
NEG_INF = -1e30
ROPE_THETA = 10000.0
GRID_W = 64
WINDOW = 128
A_HEADS, A_KV, A_HD = 16, 4, 64
B_HEADS, B_KV, B_HD = 8, 2, 128
A_Q, A_KVW = A_HEADS * A_HD, A_KV * A_HD
B_Q, B_KVW = B_HEADS * B_HD, B_KV * B_HD
IN_COLS = A_Q + 2 * A_KVW + B_Q + 2 * B_KVW
PK_HEADS, N_KEYS, PK_TOPK = 8, 128, 16
PK_DIM = 256
N_PICKS = PK_HEADS * PK_TOPK
LANES = 128
SUBLANES = 8
ROW_WORDS = D_MODEL // 2
ROW_SUB = ROW_WORDS // LANES
PICK_LANES = SUBLANES * N_PICKS
STAGE_TOKENS = 64
STAGE_BUFS = 2
LOG2E = 1.4426950408889634
VMEM_LIMIT = 56 * 1024 * 1024


def _cparams(sem):
    return pltpu.CompilerParams(dimension_semantics=sem, vmem_limit_bytes=VMEM_LIMIT)


def _const_spec(shape):
    nd = len(shape)
    return pl.BlockSpec(shape, lambda *_: (0,) * nd, pipeline_mode=pl.Buffered(1))


def _rms(x):
    return x * lax.rsqrt(jnp.mean(x * x, axis=-1, keepdims=True) + EPS)


def _inproj_kernel(x_ref, g_ref, w_ref, cs_ref, qn_ref, kn_ref,
                   qa_ref, ka_ref, va_ref, qb_ref, kb_ref, vb_ref):
    tb = x_ref.shape[0]
    hb = (_rms(x_ref[...]) * g_ref[...]).astype(BF16)
    lane = lax.broadcasted_iota(I32, (tb, LANES), 1)
    first = (lane % 64) < 32
    cos_a, sin_a, cos_b, sin_b = cs_ref[0], cs_ref[1], cs_ref[2], cs_ref[3]

    def rope(xc, cos, sin_signed):
        rot = jnp.where(first, pltpu.roll(xc, 96, 1), pltpu.roll(xc, 32, 1))
        return xc * cos + rot * sin_signed

    def proj(c0, width):
        return jnp.dot(hb, w_ref[:, c0:c0 + width], preferred_element_type=F32)

    for j in range(A_Q // 256):
        y = proj(j * 256, 256)
        for s in range(2):
            yc = rope(y[:, s * LANES:(s + 1) * LANES], cos_a, sin_a) * (A_HD ** -0.5)
            qa_ref[:, j * 256 + s * LANES:j * 256 + (s + 1) * LANES] = yc.astype(BF16)
    y = proj(A_Q, 256)
    for s in range(2):
        ka_ref[:, s * LANES:(s + 1) * LANES] = rope(y[:, s * LANES:(s + 1) * LANES], cos_a, sin_a).astype(BF16)
    va_ref[...] = proj(A_Q + A_KVW, 256).astype(BF16)
    base = A_Q + 2 * A_KVW
    for j in range(B_Q // 256):
        y = proj(base + j * 256, 256)
        for s in range(2):
            yc = _rms(y[:, s * LANES:(s + 1) * LANES]) * qn_ref[...]
            yc = rope(yc, cos_b, sin_b) * (B_HD ** -0.5 * LOG2E)
            qb_ref[:, j * 256 + s * LANES:j * 256 + (s + 1) * LANES] = yc.astype(BF16)
    y = proj(base + B_Q, 256)
    for s in range(2):
        yc = _rms(y[:, s * LANES:(s + 1) * LANES]) * kn_ref[...]
        kb_ref[:, s * LANES:(s + 1) * LANES] = rope(yc, cos_b, sin_b).astype(BF16)
    y = proj(base + B_Q + B_KVW, 256).astype(BF16)
    ones = jnp.ones((tb, B_HD), BF16)
    for kv in range(B_KV):
        vb_ref[:, 2 * kv * B_HD:(2 * kv + 1) * B_HD] = y[:, kv * B_HD:(kv + 1) * B_HD]
        vb_ref[:, (2 * kv + 1) * B_HD:(2 * kv + 2) * B_HD] = ones


def _inproj(x2d, g_mix, w_in, cs, qn, kn, seq, tb):
    t = x2d.shape[0]
    ns = seq // tb
    row = lambda w: pl.BlockSpec((tb, w), lambda i: (i, 0))
    out = lambda w: jax.ShapeDtypeStruct((t, w), BF16)
    return pl.pallas_call(
        _inproj_kernel,
        grid=(t // tb,),
        in_specs=[row(D_MODEL), _const_spec((1, D_MODEL)), _const_spec((D_MODEL, IN_COLS)),
                  pl.BlockSpec((4, tb, LANES), lambda i: (0, i % ns, 0)),
                  _const_spec((1, LANES)), _const_spec((1, LANES))],
        out_specs=[row(A_Q), row(A_KVW), row(A_KVW), row(B_Q), row(B_KVW), row(2 * B_KVW)],
        out_shape=[out(A_Q), out(A_KVW), out(A_KVW), out(B_Q), out(B_KVW), out(2 * B_KVW)],
        compiler_params=_cparams(("parallel",)),
        name="inproj",
    )(x2d, g_mix, w_in, cs, qn, kn)


def _window_kernel(sink_ref, q_ref, kp_ref, kc_ref, kn_ref, vp_ref, vc_ref, vn_ref, o_ref):
    i = pl.program_id(1)
    nb = pl.num_programs(1)
    blk = q_ref.shape[0]
    r = lax.broadcasted_iota(I32, (blk, 3 * blk), 0)
    c = lax.broadcasted_iota(I32, (blk, 3 * blk), 1)
    rel = c - blk - r
    lo_ok = jnp.where(i > 0, 0, blk)
    hi_ok = jnp.where(i < nb - 1, 3 * blk, 2 * blk)
    ok = (jnp.abs(rel) <= WINDOW) & (c >= lo_ok) & (c < hi_ok)
    k3 = jnp.concatenate([kp_ref[...], kc_ref[...], kn_ref[...]], axis=0)
    v3 = jnp.concatenate([vp_ref[...], vc_ref[...], vn_ref[...]], axis=0)
    group = A_HEADS // A_KV
    ss = []
    for h in range(A_HEADS):
        kv = h // group
        qh = q_ref[:, h * A_HD:(h + 1) * A_HD]
        kh = k3[:, kv * A_HD:(kv + 1) * A_HD]
        s = lax.dot_general(qh, kh, (((1,), (1,)), ((), ())), preferred_element_type=F32)
        ss.append(jnp.where(ok, s, NEG_INF))
    ms = [jnp.maximum(jnp.max(s, axis=1, keepdims=True), sink_ref[h]) for h, s in enumerate(ss)]
    ps = [jnp.exp(s - m) for s, m in zip(ss, ms)]
    dens = [jnp.sum(p, axis=1, keepdims=True) + jnp.exp(sink_ref[h] - m) for h, (p, m) in enumerate(zip(ps, ms))]
    outs = []
    for h in range(A_HEADS):
        kv = h // group
        vh = v3[:, kv * A_HD:(kv + 1) * A_HD]
        o = jnp.dot(ps[h].astype(BF16), vh, preferred_element_type=F32) / dens[h]
        outs.append(o.astype(BF16))
    o_ref[...] = jnp.concatenate(outs, axis=1)


def _window(sink, qa, ka, va, blk=128):
    b, s, _ = qa.shape
    nb = s // blk
    qspec = pl.BlockSpec((None, blk, A_Q), lambda bi, i: (bi, i, 0))
    prev = pl.BlockSpec((None, blk, A_KVW), lambda bi, i: (bi, jnp.maximum(i - 1, 0), 0))
    cur = pl.BlockSpec((None, blk, A_KVW), lambda bi, i: (bi, i, 0))
    nxt = pl.BlockSpec((None, blk, A_KVW), lambda bi, i: (bi, jnp.minimum(i + 1, nb - 1), 0))
    return pl.pallas_call(
        _window_kernel,
        grid=(b, nb),
        in_specs=[pl.BlockSpec(memory_space=pltpu.SMEM), qspec, prev, cur, nxt, prev, cur, nxt],
        out_specs=qspec,
        out_shape=jax.ShapeDtypeStruct((b, s, A_Q), BF16),
        compiler_params=_cparams(("parallel", "parallel")),
        name="window",
    )(sink, qa, ka, ka, ka, va, va, va)


def _dense_kernel(q_ref, k_ref, v_ref, o_ref, m_sc, acc_sc, *, tk, rb):
    tq = q_ref.shape[0]
    group = B_HEADS // B_KV
    qs = [q_ref[i * rb:(i + 1) * rb, g * B_HD:(g + 1) * B_HD] for g in range(group) for i in range(tq // rb)]
    m_sc[...] = jnp.full(m_sc.shape, -jnp.inf, F32)
    acc_sc[...] = jnp.zeros(acc_sc.shape, F32)

    def body(j, carry):
        off = pl.multiple_of(j * tk, tk)
        k = k_ref[pl.ds(off, tk), :]
        v = v_ref[pl.ds(off, tk), :]
        for n, q in enumerate(qs):
            rows = slice(n * rb, (n + 1) * rb)
            s = lax.dot_general(q, k, (((1,), (1,)), ((), ())), preferred_element_type=F32)
            m_prev = m_sc[rows]
            m_next = jnp.maximum(m_prev, jnp.max(s, axis=1, keepdims=True))
            p = jnp.exp2(s - jnp.concatenate([m_next] * (tk // LANES), axis=1))
            alpha = jnp.exp2(m_prev - m_next)
            acc_sc[rows] = jnp.concatenate([alpha, alpha], axis=1) * acc_sc[rows] + jnp.dot(
                p.astype(BF16), v, preferred_element_type=F32)
            m_sc[rows] = m_next
        return carry

    lax.fori_loop(0, k_ref.shape[0] // tk, body, 0)
    for g in range(group):
        acc = acc_sc[g * tq:(g + 1) * tq]
        o_ref[:, g * B_HD:(g + 1) * B_HD] = (acc[:, :B_HD] / acc[:, B_HD:]).astype(BF16)


def _dense(qb, kb, vb1, tq=1024, tk=2048, rb=256):
    b, s, _ = qb.shape
    tq, tk = min(tq, s), min(tk, s)
    rb = min(rb, tq)
    group = B_HEADS // B_KV
    gw = group * B_HD
    qspec = pl.BlockSpec((None, tq, gw), lambda bi, kv, i: (bi, i, kv))
    kspec = pl.BlockSpec((None, s, B_HD), lambda bi, kv, i: (bi, 0, kv))
    vspec = pl.BlockSpec((None, s, 2 * B_HD), lambda bi, kv, i: (bi, 0, kv))
    return pl.pallas_call(
        functools.partial(_dense_kernel, tk=tk, rb=rb),
        grid=(b, B_KV, s // tq),
        in_specs=[qspec, kspec, vspec],
        out_specs=qspec,
        out_shape=jax.ShapeDtypeStruct((b, s, B_Q), BF16),
        scratch_shapes=[pltpu.VMEM((group * tq, B_HD), F32), pltpu.VMEM((group * tq, 2 * B_HD), F32)],
        compiler_params=_cparams(("parallel", "parallel", "parallel")),
        name="dense",
    )(qb, kb, vb1)


def _mixout_kernel(x_ref, oa_ref, ob_ref, gmix_ref, wg_ref, bg_ref, woa_ref, wob_ref, wout_ref,
                   gffn_ref, wq_ref, x2_ref, h2_ref, qry_ref):
    x = x_ref[...]
    hb = (_rms(x) * gmix_ref[...]).astype(BF16)
    gates = jax.nn.sigmoid(jnp.dot(hb, wg_ref[...], preferred_element_type=F32) + bg_ref[...])
    ya = jnp.dot(oa_ref[...], woa_ref[...], preferred_element_type=F32)
    yb = jnp.dot(ob_ref[...], wob_ref[...], preferred_element_type=F32)
    merged = (gates[:, :D_MODEL] * ya + gates[:, D_MODEL:] * yb).astype(BF16)
    x2 = x + jnp.dot(merged, wout_ref[...], preferred_element_type=F32)
    x2_ref[...] = x2
    h2 = _rms(x2) * gffn_ref[...]
    h2_ref[...] = h2
    qry_ref[...] = jnp.dot(h2.astype(BF16), wq_ref[...], preferred_element_type=F32).astype(BF16)


def _mixout(x2d, oa, ob, gmix, wg, bg, woa, wob, wout, gffn, wq, tb):
    t = x2d.shape[0]
    row = lambda w: pl.BlockSpec((tb, w), lambda i: (i, 0))
    qw = PK_HEADS * PK_DIM
    return pl.pallas_call(
        _mixout_kernel,
        grid=(t // tb,),
        in_specs=[row(D_MODEL), row(A_Q), row(B_Q), _const_spec((1, D_MODEL)),
                  _const_spec((D_MODEL, 2 * D_MODEL)), _const_spec((1, 2 * D_MODEL)),
                  _const_spec((A_Q, D_MODEL)), _const_spec((B_Q, D_MODEL)), _const_spec((D_MODEL, D_MODEL)),
                  _const_spec((1, D_MODEL)), _const_spec((D_MODEL, qw))],
        out_specs=[row(D_MODEL), row(D_MODEL), row(qw)],
        out_shape=[jax.ShapeDtypeStruct((t, D_MODEL), F32), jax.ShapeDtypeStruct((t, D_MODEL), F32),
                   jax.ShapeDtypeStruct((t, qw), BF16)],
        compiler_params=_cparams(("parallel",)),
        name="mixout",
    )(x2d, oa, ob, gmix, wg, bg, woa, wob, wout, gffn, wq)


def _top_rows_multi(streams, k):
    big = float(PK_TOPK * PK_TOPK)
    vals = [v for v, _ in streams]
    ids = [i for _, i in streams]
    top_v = [[] for _ in streams]
    top_i = [[] for _ in streams]
    for _ in range(k):
        ms = [jnp.max(v, axis=0, keepdims=True) for v in vals]
        idxs = [jnp.min(jnp.where(v == m, i, big), axis=0, keepdims=True) for v, m, i in zip(vals, ms, ids)]
        vals = [jnp.where(i == idx, -jnp.inf, v) for v, i, idx in zip(vals, ids, idxs)]
        for n, (m, idx) in enumerate(zip(ms, idxs)):
            top_v[n].append(m)
            top_i[n].append(idx)
    return [(jnp.concatenate(v, axis=0), jnp.concatenate(i, axis=0)) for v, i in zip(top_v, top_i)]


def _top_rows(vals, ids, k):
    return _top_rows_multi([(vals, ids)], k)[0]


def _topk_kernel(qry_ref, keys_ref, idx_ref, gate_ref):
    tb = qry_ref.shape[0]
    row_id = lambda n: lax.broadcasted_iota(I32, (n, tb), 0).astype(F32)
    scores = []
    for p in range(2):
        q = qry_ref[:, p * N_KEYS:(p + 1) * N_KEYS]
        sc = lax.dot_general(keys_ref[0, p], q, (((1,), (1,)), ((), ())), preferred_element_type=F32)
        scores.append((sc, row_id(N_KEYS)))
    (s1, i1), (s2, i2) = _top_rows_multi(scores, PK_TOPK)
    half = PK_TOPK // 2
    cand = [s1[0:1] + s2]
    cpos = [row_id(PK_TOPK)]
    cidx = [i1[0:1] * float(N_KEYS) + i2]
    for a in range(1, half):
        cand.append(s1[a:a + 1] + s2[:half])
        cpos.append(row_id(half) + float(a * PK_TOPK))
        cidx.append(i1[a:a + 1] * float(N_KEYS) + i2[:half])
    cand.append(s1[half:] + s2[0:1])
    cpos.append((row_id(half) + float(half)) * float(PK_TOPK))
    cidx.append(i1[half:] * float(N_KEYS) + i2[0:1])
    cand, cpos, cidx = (jnp.concatenate(x, axis=0) for x in (cand, cpos, cidx))
    best, pos = _top_rows(cand, cpos, PK_TOPK)
    picked = [jnp.max(jnp.where(cpos == pos[j:j + 1], cidx, -1.0), axis=0, keepdims=True)
              for j in range(PK_TOPK)]
    idx_ref[...] = jnp.concatenate(picked, axis=0).astype(I32)
    e = jnp.exp(best - best[0:1])
    gate_ref[...] = e / jnp.sum(e, axis=0, keepdims=True)


def _topk(qry, keys, tb):
    t = qry.shape[0]
    return pl.pallas_call(
        _topk_kernel,
        grid=(t // tb, PK_HEADS),
        in_specs=[pl.BlockSpec((tb, PK_DIM), lambda i, h: (i, h)),
                  pl.BlockSpec((1, 2, N_KEYS, PK_DIM // 2), lambda i, h: (h, 0, 0, 0))],
        out_specs=[pl.BlockSpec((PK_TOPK, tb), lambda i, h: (h, i)),
                   pl.BlockSpec((PK_TOPK, tb), lambda i, h: (h, i))],
        out_shape=[jax.ShapeDtypeStruct((N_PICKS, t), I32), jax.ShapeDtypeStruct((N_PICKS, t), F32)],
        compiler_params=_cparams(("parallel", "parallel")),
        name="topk",
    )(qry, keys)


def _pack_table(w):
    bits = lax.bitcast_convert_type(w.astype(BF16), jnp.uint16).astype(jnp.uint32)
    bits = bits.reshape(w.shape[0], ROW_SUB, 2, LANES)
    packed = bits[:, :, 0] | (bits[:, :, 1] << 16)
    return lax.bitcast_convert_type(packed, I32).reshape(w.shape[0] * ROW_SUB, LANES)


def _split3(x):
    hi = x.astype(BF16).astype(F32)
    r1 = x - hi
    mid = r1.astype(BF16).astype(F32)
    lo = (r1 - mid).astype(BF16).astype(F32)
    return hi, mid, lo


def _load_table(tab_hbm, tab_vmem, sem):
    @pl.when(pl.program_id(0) == 0)
    def _():
        cp = pltpu.make_async_copy(tab_hbm, tab_vmem, sem)
        cp.start()
        cp.wait()


def _weights(stage, b, u, tab):
    rows = [tab[pl.ds(pl.multiple_of(stage[b, u * N_PICKS + k], ROW_SUB), ROW_SUB), :] for k in range(N_PICKS)]
    return pltpu.bitcast(jnp.concatenate(rows, axis=0), BF16)


def _staged_token_loop(rows_hbm, stage, sem, tb, token):
    step = pl.program_id(0)
    per_step = tb // STAGE_TOKENS
    total = pl.num_programs(0) * per_step
    ahead = STAGE_BUFS - 1

    def copy(g, b):
        return pltpu.make_async_copy(rows_hbm.at[g], stage.at[b], sem.at[b])

    @pl.when(step == 0)
    def _():
        for g in range(ahead):
            copy(g, g).start()

    def body(i, carry):
        for b in range(STAGE_BUFS):
            j = STAGE_BUFS * i + b
            g = step * per_step + j
            copy(g, b).wait()

            @pl.when(g + ahead < total)
            def _():
                copy(g + ahead, (b + ahead) % STAGE_BUFS).start()

            for u in range(STAGE_TOKENS):
                token(j * STAGE_TOKENS + u, b, u)
        return carry

    lax.fori_loop(0, per_step // STAGE_BUFS, body, 0)


def _stage_scratch():
    return [pltpu.SMEM((STAGE_BUFS, STAGE_TOKENS * N_PICKS), I32), pltpu.SemaphoreType.DMA((STAGE_BUFS,))]


def _lhs3(x):
    hi, mid, lo = _split3(x)
    return jnp.concatenate([hi, mid, lo, jnp.zeros_like(x)], axis=0).astype(BF16)


def _diag_mask():
    shape = (SUBLANES, PICK_LANES)
    return (lax.broadcasted_iota(I32, shape, 1) % SUBLANES) == lax.broadcasted_iota(I32, shape, 0)


def _group_allsum(x):
    n = x.shape[1]
    lane = lax.broadcasted_iota(I32, x.shape, 1)
    dist = 1
    while dist < SUBLANES:
        x = x + jnp.where((lane & dist) == 0, pltpu.roll(x, n - dist, 1), pltpu.roll(x, dist, 1))
        dist *= 2
    return x


def _peer_u_kernel(rows_hbm, h_ref, gate_ref, tab_hbm, c_ref, tab, dbuf, sem, stage, stage_sem):
    _load_table(tab_hbm, tab, sem)
    diag = _diag_mask()

    def token(t, b, u):
        w = _weights(stage, b, u, tab)
        r = lax.dot_general(_lhs3(h_ref[t]), w, (((1,), (1,)), ((), ())), preferred_element_type=F32)
        rc = r[0:8] + r[8:16] + r[16:24]
        dbuf[pl.ds(t, 1), :] = jnp.sum(jnp.where(diag, rc, 0.0), axis=0, keepdims=True)

    _staged_token_loop(rows_hbm, stage, stage_sem, h_ref.shape[0], token)
    d = _group_allsum(dbuf[...])
    shape = (N_PICKS, PICK_LANES)
    spread = (lax.broadcasted_iota(I32, shape, 1) // SUBLANES == lax.broadcasted_iota(I32, shape, 0)).astype(BF16)
    g8 = sum(jnp.dot(p.astype(BF16), spread, preferred_element_type=F32) for p in _split3(gate_ref[...]))
    c_ref[...] = g8 * (0.5 * d * (1.0 + lax.erf(d * (0.5 ** 0.5))))


def _peer_u(rows, h3, gate, tab, tb):
    t = h3.shape[0]
    return pl.pallas_call(
        _peer_u_kernel,
        grid=(t // tb,),
        in_specs=[pl.BlockSpec(memory_space=pl.ANY),
                  pl.BlockSpec((tb, SUBLANES, LANES), lambda i: (i, 0, 0)),
                  pl.BlockSpec((tb, N_PICKS), lambda i: (i, 0)),
                  pl.BlockSpec(memory_space=pl.ANY)],
        out_specs=pl.BlockSpec((tb, PICK_LANES), lambda i: (i, 0)),
        out_shape=jax.ShapeDtypeStruct((t, PICK_LANES), F32),
        scratch_shapes=[pltpu.VMEM(tab.shape, I32), pltpu.VMEM((tb, PICK_LANES), F32),
                        pltpu.SemaphoreType.DMA(())] + _stage_scratch(),
        compiler_params=_cparams(("arbitrary",)),
        name="peer_u",
    )(rows, h3, gate, tab)


def _peer_v_kernel(rows_hbm, c_ref, tab_hbm, out_ref, tab, sem, stage, stage_sem):
    _load_table(tab_hbm, tab, sem)
    diag = _diag_mask()

    def token(t, b, u):
        w = _weights(stage, b, u, tab)
        c8 = jnp.where(diag, c_ref[pl.ds(t, 1), :], 0.0)
        r = jnp.dot(_lhs3(c8), w, preferred_element_type=F32)
        out_ref[t] = r[0:8] + r[8:16] + r[16:24]

    _staged_token_loop(rows_hbm, stage, stage_sem, out_ref.shape[0], token)


def _peer_v(rows, c8, tab, tb):
    t = c8.shape[0]
    return pl.pallas_call(
        _peer_v_kernel,
        grid=(t // tb,),
        in_specs=[pl.BlockSpec(memory_space=pl.ANY),
                  pl.BlockSpec((tb, PICK_LANES), lambda i: (i, 0)),
                  pl.BlockSpec(memory_space=pl.ANY)],
        out_specs=pl.BlockSpec((tb, SUBLANES, LANES), lambda i: (i, 0, 0)),
        out_shape=jax.ShapeDtypeStruct((t, SUBLANES, LANES), F32),
        scratch_shapes=[pltpu.VMEM(tab.shape, I32), pltpu.SemaphoreType.DMA(())] + _stage_scratch(),
        compiler_params=_cparams(("arbitrary",)),
        name="peer_v",
    )(rows, c8, tab)


def _peer(idx_t, gate_t, h2, tab_u, tab_v, tb):
    t = h2.shape[0]
    assert tb % (STAGE_BUFS * STAGE_TOKENS) == 0 and t % tb == 0
    rows = (idx_t.T * ROW_SUB).reshape(t // STAGE_TOKENS, STAGE_TOKENS * N_PICKS)
    c8 = _peer_u(rows, h2.reshape(t, SUBLANES, LANES), gate_t.T, tab_u, tb)
    out = _peer_v(rows, c8, tab_v, tb)
    return out.reshape(t, D_MODEL)


def _final_kernel(x_ref, p_ref, g_ref, y_ref):
    y_ref[...] = _rms(x_ref[...] + p_ref[...]) * g_ref[...]


def _final(x2, peer, g, tb):
    t = x2.shape[0]
    row = pl.BlockSpec((tb, D_MODEL), lambda i: (i, 0))
    return pl.pallas_call(
        _final_kernel,
        grid=(t // tb,),
        in_specs=[row, row, _const_spec((1, D_MODEL))],
        out_specs=row,
        out_shape=jax.ShapeDtypeStruct((t, D_MODEL), F32),
        compiler_params=_cparams(("parallel",)),
        name="final",
    )(x2, peer, g)


def _rope_tables(seq):
    def cos_sin(pos, dim):
        inv = ROPE_THETA ** (-jnp.arange(0, dim, 2, dtype=F32) / dim)
        ang = pos.astype(F32)[:, None] * inv[None, :]
        ang = jnp.concatenate([ang, ang], axis=-1)
        return jnp.cos(ang), jnp.sin(ang)

    sign = jnp.where(jnp.arange(64) < 32, -1.0, 1.0).astype(F32)
    ca, sa = cos_sin(jnp.arange(seq), A_HD)
    rows = seq // GRID_W
    cr, sr = cos_sin(jnp.repeat(jnp.arange(rows), GRID_W), B_HD // 2)
    cc, sc = cos_sin(jnp.tile(jnp.arange(GRID_W), rows), B_HD // 2)
    cos_a = jnp.concatenate([ca, ca], axis=-1)
    sin_a = jnp.concatenate([sa * sign, sa * sign], axis=-1)
    cos_b = jnp.concatenate([cr, cc], axis=-1)
    sin_b = jnp.concatenate([sr * sign, sc * sign], axis=-1)
    return jnp.stack([cos_a, sin_a, cos_b, sin_b])


def _trunk(x, w, tb_proj=256, tb_peer=128):
    b, s, d = x.shape
    t = b * s
    x2d = x.reshape(t, d)
    cs = _rope_tables(s)
    qa, ka, va, qb, kb, vb1 = _inproj(x2d, w["g_mix"], w["w_in"], cs, w["qn"], w["kn"], s, min(tb_proj, s))
    sh = lambda a: a.reshape(b, s, a.shape[-1])
    oa = _window(w["sink"], sh(qa), sh(ka), sh(va))
    ob = _dense(sh(qb), sh(kb), sh(vb1))
    x2, h2, qry = _mixout(x2d, oa.reshape(t, A_Q), ob.reshape(t, B_Q), w["g_mix"], w["w_gate"], w["b_gate"],
                          w["w_oa"], w["w_ob"], w["w_out"], w["g_ffn"], w["w_query"], tb_proj)
    idx_t, gate_t = _topk(qry, w["keys"], tb_proj)
    peer = _peer(idx_t, gate_t, h2, w["tab_u"], w["tab_v"], tb_peer)
    y = _final(x2, peer, w["g_final"], tb_proj)
    return y.reshape(b, s, d)


def kernel(x_prompt, x_sample, g_mix, w_in, w_gate, b_gate, sink, q_norm_g, k_norm_g, w_oa, w_ob, w_out,
           g_ffn, w_query, sub_keys, expert_u, expert_v, g_final):
    assert w_in.shape[0] == 1, "single-layer trunk"
    l = 0
    w = {
        "g_mix": g_mix[l][None, :], "w_in": w_in[l].astype(BF16),
        "w_gate": w_gate[l].astype(BF16), "b_gate": b_gate[l][None, :],
        "sink": sink[l], "qn": q_norm_g[l][None, :], "kn": k_norm_g[l][None, :],
        "w_oa": w_oa[l].astype(BF16), "w_ob": w_ob[l].astype(BF16), "w_out": w_out[l].astype(BF16),
        "g_ffn": g_ffn[l][None, :], "w_query": w_query[l].astype(BF16),
        "keys": sub_keys[l].astype(BF16),
        "tab_u": _pack_table(expert_u[l]), "tab_v": _pack_table(expert_v[l]),
        "g_final": g_final[None, :],
    }
    return _trunk(x_prompt, w), _trunk(x_sample, w)
```
